```python
import math
import jax, jax.numpy as jnp
from jax import lax
import numpy as np

D_MODEL = 2048
BATCH = 32
SEQ = 256
DEPTH = 4
DEC_BATCH = 4
DEC_SEQ = 2048
PAST_LEN = 512

GRID_W = 64
N_MIXERS = 3
N_SSD = (DEPTH + 2) // 3
N_ATTN = (DEPTH + 1) // 3
N_MLSTM = DEPTH // 3
EPS = 1e-6
CHUNK = 128

SSD_D_INNER = 2 * D_MODEL
SSD_HEAD_DIM = 64
SSD_HEADS = SSD_D_INNER // SSD_HEAD_DIM
SSD_GROUPS = 8
SSD_D_STATE = 128
SSD_CONV_W = 3
SSD_CONV_DIM = SSD_D_INNER + 2 * SSD_GROUPS * SSD_D_STATE
SSD_IN_DIM = SSD_D_INNER + SSD_CONV_DIM + 2 * SSD_HEADS

ATTN_HEAD_DIM = 128
ATTN_HEADS = D_MODEL // ATTN_HEAD_DIM
ATTN_KV_HEADS = 4
Q_BLOCK = 128
ROPE_THETA = 10000.0

MLSTM_HEADS = 8
MLSTM_DK = D_MODEL // 2 // MLSTM_HEADS
MLSTM_DV = D_MODEL // MLSTM_HEADS
MLSTM_IN_DIM = 2 * MLSTM_HEADS * MLSTM_DK + 2 * MLSTM_HEADS * MLSTM_DV + 4 * MLSTM_HEADS

FFN_HIDDEN = ((8 * D_MODEL + 3 * 256 - 1) // (3 * 256)) * 256

kernel_name = 'hybrid_flow_ssd_gqa_mlstm_step'

F32 = jnp.float32


def rms_norm(x, w):
    xf = x.astype(F32)
    y = xf * lax.rsqrt(jnp.mean(xf * xf, axis=-1, keepdims=True) + EPS)
    return (y * w.astype(F32)).astype(x.dtype)


def group_rms_norm(y, w, groups):
    shp = y.shape
    yg = y.reshape(shp[:-1] + (groups, shp[-1] // groups))
    return rms_norm(yg, w.reshape(groups, -1)).reshape(shp)


def modulate(x, shift, scale):
    return x * (1 + scale) + shift


def swiglu(u, w_gate, w_up, w_down):
    return (jax.nn.silu(u @ w_gate) * (u @ w_up)) @ w_down


def depthwise_conv(x, w, b):
    pad = SSD_CONV_W // 2
    y = lax.conv_general_dilated(x, w[:, None, :], window_strides=(1,), padding=[(pad, pad)],
                                 dimension_numbers=('NWC', 'WIO', 'NWC'),
                                 feature_group_count=x.shape[-1])
    return y + b


def ssd_scan(x, dt, A, Bm, Cm, h0):
    b, L, H, P = x.shape
    G, N = Bm.shape[2], Bm.shape[3]
    R = H // G
    nc = L // CHUNK
    xc = x.reshape(b, nc, CHUNK, G, R, P)
    dtc = dt.reshape(b, nc, CHUNK, G, R)
    Bc = Bm.reshape(b, nc, CHUNK, G, N)
    Cc = Cm.reshape(b, nc, CHUNK, G, N)
    a_cs = jnp.cumsum(dtc * A.reshape(G, R), axis=2)
    seg = a_cs[:, :, :, None] - a_cs[:, :, None, :]
    mask = jnp.tril(jnp.ones((CHUNK, CHUNK), bool))[:, :, None, None]
    decay = jnp.exp(jnp.where(mask, seg, -jnp.inf))
    CB = jnp.einsum('bctgn,bcsgn->bctsg', Cc, Bc)
    xdt = xc * dtc[..., None]
    y_intra = jnp.einsum('bctsgr,bcsgrp->bctgrp', CB[..., None] * decay, xdt)
    decay_end = jnp.exp(a_cs[:, :, -1:] - a_cs)
    S = jnp.einsum('bcsgn,bcsgrp->bcgrpn', Bc, xdt * decay_end[..., None]).astype(F32)
    tot = jnp.exp(a_cs[:, :, -1])

    def step(h, inp):
        tot_c, S_c = inp
        return tot_c[..., None, None] * h + S_c, h

    h_init = h0.astype(F32).reshape(b, G, R, P, N)
    hT, h_prev = lax.scan(step, h_init, (jnp.moveaxis(tot, 1, 0), jnp.moveaxis(S, 1, 0)))
    h_prev = jnp.moveaxis(h_prev, 0, 1)
    y_inter = jnp.einsum('bctgn,bcgrpn->bctgrp', Cc, h_prev) * jnp.exp(a_cs)[..., None]
    y = (y_intra + y_inter).reshape(b, L, H, P).astype(x.dtype)
    return y, hT.reshape(b, H, P, N).astype(h0.dtype)


def ssd_mixer(u, h0, w_in, conv_w, conv_b, dt_bias, a_log, d_skip, norm_w, w_out):
    b, L, _ = u.shape
    GN = SSD_GROUPS * SSD_D_STATE
    proj = u @ w_in
    z = proj[..., :SSD_D_INNER]
    xbc = jax.nn.silu(depthwise_conv(proj[..., SSD_D_INNER:SSD_D_INNER + SSD_CONV_DIM], conv_w, conv_b))
    dt_raw = proj[..., SSD_D_INNER + SSD_CONV_DIM:].astype(F32).reshape(b, L, 2, SSD_HEADS)
    xs = xbc[..., :SSD_D_INNER].reshape(b, L, SSD_HEADS, SSD_HEAD_DIM)
    Bm = xbc[..., SSD_D_INNER:SSD_D_INNER + GN].reshape(b, L, SSD_GROUPS, SSD_D_STATE)
    Cm = xbc[..., SSD_D_INNER + GN:].reshape(b, L, SSD_GROUPS, SSD_D_STATE)
    dt = jax.nn.softplus(dt_raw + dt_bias.astype(F32))
    A = -jnp.exp(a_log.astype(F32))
    y_f, h_f = ssd_scan(xs, dt[:, :, 0], A[0], Bm, Cm, h0[:, 0])
    y_b, h_b = ssd_scan(xs[:, ::-1], dt[:, ::-1, 1], A[1], Bm[:, ::-1], Cm[:, ::-1], h0[:, 1])
    y = y_f + y_b[:, ::-1] + d_skip[:, None] * xs
    y = group_rms_norm(y.reshape(b, L, SSD_D_INNER) * jax.nn.silu(z), norm_w, SSD_GROUPS)
    return y @ w_out, jnp.stack([h_f, h_b], axis=1)


def attn_project(u, w_qkv, q_norm, k_norm):
    b, L, _ = u.shape
    nq = ATTN_HEADS * ATTN_HEAD_DIM
    nk = ATTN_KV_HEADS * ATTN_HEAD_DIM
    qkv = u @ w_qkv
    q = rms_norm(qkv[..., :nq].reshape(b, L, ATTN_HEADS, ATTN_HEAD_DIM), q_norm)
    k = rms_norm(qkv[..., nq:nq + nk].reshape(b, L, ATTN_KV_HEADS, ATTN_HEAD_DIM), k_norm)
    v = qkv[..., nq + nk:].reshape(b, L, ATTN_KV_HEADS, ATTN_HEAD_DIM)
    return q, k, v


def axial_rope(x, rows):
    half = ATTN_HEAD_DIM // 2
    quarter = half // 2
    pos_row = jnp.repeat(jnp.arange(rows), GRID_W)
    pos_col = jnp.tile(jnp.arange(GRID_W), rows)
    inv_freq = ROPE_THETA ** (-jnp.arange(quarter, dtype=F32) / quarter)

    def rotate(xa, pos):
        ang = pos.astype(F32)[:, None] * inv_freq
        cos = jnp.cos(ang)[None, :, None, :]
        sin = jnp.sin(ang)[None, :, None, :]
        xf = xa.astype(F32)
        x1, x2 = xf[..., :quarter], xf[..., quarter:]
        return jnp.concatenate([x1 * cos - x2 * sin, x2 * cos + x1 * sin], axis=-1)

    out = jnp.concatenate([rotate(x[..., :half], pos_row), rotate(x[..., half:], pos_col)], axis=-1)
    return out.astype(x.dtype)


def blocked_attention(q, k, v):
    b, Lq = q.shape[:2]
    grp = ATTN_HEADS // ATTN_KV_HEADS
    nb = Lq // Q_BLOCK
    qb = jnp.moveaxis(q.reshape(b, nb, Q_BLOCK, ATTN_KV_HEADS, grp, ATTN_HEAD_DIM), 1, 0)
    scale = ATTN_HEAD_DIM ** -0.5

    def one_block(qblk):
        s = jnp.einsum('bqkgd,bskd->bkgqs', qblk, k).astype(F32) * scale
        p = jax.nn.softmax(s, axis=-1).astype(v.dtype)
        return jnp.einsum('bkgqs,bskd->bqkgd', p, v)

    o = lax.map(one_block, qb)
    return jnp.moveaxis(o, 0, 1).reshape(b, Lq, ATTN_HEADS * ATTN_HEAD_DIM)


def mlstm_scan(q, k, v, i_raw, log_f, C0, n0, m0):
    b, L, H, _ = q.shape
    DV = v.shape[-1]
    nc = L // CHUNK

    def to_chunks(t):
        return jnp.moveaxis(t.reshape((b, nc, CHUNK) + t.shape[2:]), 1, 0)

    mask = jnp.tril(jnp.ones((CHUNK, CHUNK), bool))[None, :, :, None]

    def step(carry, inp):
        C, n, m = carry
        qc, kc, vc, ic, fc = inp
        bcum = jnp.cumsum(fc, axis=1)
        Dm = jnp.where(mask, bcum[:, :, None] - bcum[:, None, :] + ic[:, None, :], -jnp.inf)
        m_inter = bcum + m[:, None]
        m_t = jnp.maximum(m_inter, jnp.max(Dm, axis=2))
        W = jnp.exp(Dm - m_t[:, :, None])
        S = jnp.einsum('bthd,bshd->btsh', qc, kc) * W
        inter = jnp.exp(m_inter - m_t)
        num = jnp.einsum('btsh,bshv->bthv', S, vc) + inter[..., None] * jnp.einsum('bthd,bhdv->bthv', qc, C)
        den = jnp.sum(S, axis=2) + inter * jnp.einsum('bthd,bhd->bth', qc, n)
        h = num / jnp.maximum(jnp.abs(den), jnp.exp(-m_t))[..., None]
        bQ = bcum[:, -1]
        log_w = bQ[:, None] - bcum + ic
        m_new = jnp.maximum(bQ + m, jnp.max(log_w, axis=1))
        w = jnp.exp(log_w - m_new[:, None])
        carry_decay = jnp.exp(bQ + m - m_new)
        wk = w[..., None] * kc
        C_new = carry_decay[..., None, None] * C + jnp.einsum('bshd,bshv->bhdv', wk, vc)
        n_new = carry_decay[..., None] * n + jnp.sum(wk, axis=1)
        return (C_new, n_new, m_new), h

    init = (C0.astype(F32), n0.astype(F32), m0.astype(F32))
    (CT, nT, mT), h = lax.scan(step, init, (to_chunks(q), to_chunks(k), to_chunks(v),
                                             to_chunks(i_raw), to_chunks(log_f)))
    h = jnp.moveaxis(h, 0, 1).reshape(b, L, H, DV).astype(q.dtype)
    return h, CT.astype(C0.dtype), nT.astype(n0.dtype), mT.astype(m0.dtype)


def mlstm_mixer(u, C0, n0, m0, w_in, b_gates, head_norm, w_out):
    b, L, _ = u.shape
    H, DK, DV = MLSTM_HEADS, MLSTM_DK, MLSTM_DV
    o1, o2 = H * DK, 2 * H * DK
    o3, o4 = o2 + H * DV, o2 + 2 * H * DV
    proj = u @ w_in
    q = proj[..., :o1].reshape(b, L, H, DK) * (DK ** -0.5)
    k = proj[..., o1:o2].reshape(b, L, H, DK)
    v = proj[..., o2:o3].reshape(b, L, H, DV)
    o_gate = jax.nn.sigmoid(proj[..., o3:o4])
    gates = proj[..., o4:].astype(F32).reshape(b, L, 2, 2, H) + b_gates.astype(F32)
    i_raw = gates[:, :, :, 0]
    log_f = jax.nn.log_sigmoid(gates[:, :, :, 1])
    h_f, C_f, n_f, m_f = mlstm_scan(q, k, v, i_raw[:, :, 0], log_f[:, :, 0], C0[:, 0], n0[:, 0], m0[:, 0])
    h_b, C_b, n_b, m_b = mlstm_scan(q[:, ::-1], k[:, ::-1], v[:, ::-1], i_raw[:, ::-1, 1], log_f[:, ::-1, 1],
                                    C0[:, 1], n0[:, 1], m0[:, 1])
    h = rms_norm(h_f + h_b[:, ::-1], head_norm).reshape(b, L, H * DV) * o_gate
    return (h @ w_out, jnp.stack([C_f, C_b], axis=1), jnp.stack([n_f, n_b], axis=1),
            jnp.stack([m_f, m_b], axis=1))


def setup_inputs(seed: int = 0) -> dict:
    key = jax.random.key(seed)
    ks = iter(jax.random.split(key, 64))
    D = D_MODEL
    inv = D ** -0.5

    def nrm(shape, scale):
        return jax.random.normal(next(ks), shape, F32) * scale

    x_prompt = nrm((BATCH, SEQ, D), 1.0)
    x_sample = nrm((DEC_BATCH, DEC_SEQ, D), 1.0)
    c = nrm((DEC_BATCH, D), 1.0)
    state_ssd = nrm((DEC_BATCH, N_SSD, 2, SSD_HEADS, SSD_HEAD_DIM, SSD_D_STATE), 0.1)
    cache_attn_k = nrm((DEC_BATCH, N_ATTN, PAST_LEN, ATTN_KV_HEADS, ATTN_HEAD_DIM), 1.0)
    cache_attn_v = nrm((DEC_BATCH, N_ATTN, PAST_LEN, ATTN_KV_HEADS, ATTN_HEAD_DIM), 1.0)
    state_mlstm_C = nrm((DEC_BATCH, N_MLSTM, 2, MLSTM_HEADS, MLSTM_DK, MLSTM_DV), 0.1)
    state_mlstm_n = nrm((DEC_BATCH, N_MLSTM, 2, MLSTM_HEADS, MLSTM_DK), 0.5)
    state_mlstm_m = nrm((DEC_BATCH, N_MLSTM, 2, MLSTM_HEADS), 1.0)
    c_ctx = nrm((D,), 1.0)
    ada_w = nrm((DEPTH, D, 6 * D), 0.3 * inv)
    ada_b = nrm((DEPTH, 6 * D), 0.02)
    norm_mix_w = 1.0 + nrm((DEPTH, D), 0.02)
    norm_ffn_w = 1.0 + nrm((DEPTH, D), 0.02)
    ffn_w_gate = nrm((DEPTH, D, FFN_HIDDEN), inv)
    ffn_w_up = nrm((DEPTH, D, FFN_HIDDEN), inv)
    ffn_w_down = nrm((DEPTH, FFN_HIDDEN, D), FFN_HIDDEN ** -0.5)
    ssd_w_in = jnp.concatenate([nrm((N_SSD, D, SSD_D_INNER + SSD_CONV_DIM), inv),
                                nrm((N_SSD, D, 2 * SSD_HEADS), 0.1 * inv)], axis=-1)
    ssd_conv_w = nrm((N_SSD, SSD_CONV_W, SSD_CONV_DIM), SSD_CONV_W ** -0.5)
    ssd_conv_b = nrm((N_SSD, SSD_CONV_DIM), 0.02)
    dt0 = jnp.exp(jax.random.uniform(next(ks), (N_SSD, 2, SSD_HEADS), F32,
                                     minval=math.log(1e-3), maxval=math.log(1e-1)))
    ssd_dt_bias = dt0 + jnp.log(-jnp.expm1(-dt0))
    ssd_a_log = jnp.log(jax.random.uniform(next(ks), (N_SSD, 2, SSD_HEADS), F32, minval=1.0, maxval=16.0))
    ssd_d = 1.0 + nrm((N_SSD, SSD_HEADS), 0.1)
    ssd_norm_w = 1.0 + nrm((N_SSD, SSD_D_INNER), 0.02)
    ssd_w_out = nrm((N_SSD, SSD_D_INNER, D), SSD_D_INNER ** -0.5)
    attn_w_qkv = nrm((N_ATTN, D, (ATTN_HEADS + 2 * ATTN_KV_HEADS) * ATTN_HEAD_DIM), inv)
    attn_q_norm = 1.0 + nrm((N_ATTN, ATTN_HEAD_DIM), 0.02)
    attn_k_norm = 1.0 + nrm((N_ATTN, ATTN_HEAD_DIM), 0.02)
    attn_w_out = nrm((N_ATTN, ATTN_HEADS * ATTN_HEAD_DIM, D), (ATTN_HEADS * ATTN_HEAD_DIM) ** -0.5)
    mlstm_w_in = jnp.concatenate([nrm((N_MLSTM, D, MLSTM_IN_DIM - 4 * MLSTM_HEADS), inv),
                                  nrm((N_MLSTM, D, 4 * MLSTM_HEADS), 0.1 * inv)], axis=-1)
    i_bias = nrm((N_MLSTM, 2, 1, MLSTM_HEADS), 0.1)
    f_bias = jnp.linspace(3.0, 6.0, MLSTM_HEADS, dtype=F32) + nrm((N_MLSTM, 2, 1, MLSTM_HEADS), 0.1)
    mlstm_b_gates = jnp.concatenate([i_bias, f_bias], axis=2)
    mlstm_head_norm = 1.0 + nrm((N_MLSTM, MLSTM_HEADS, MLSTM_DV), 0.02)
    mlstm_w_out = nrm((N_MLSTM, MLSTM_HEADS * MLSTM_DV, D), (MLSTM_HEADS * MLSTM_DV) ** -0.5)
    return {'x_prompt': x_prompt, 'x_sample': x_sample, 'c': c,
            'state_ssd': state_ssd, 'cache_attn_k': cache_attn_k, 'cache_attn_v': cache_attn_v,
            'state_mlstm_C': state_mlstm_C, 'state_mlstm_n': state_mlstm_n, 'state_mlstm_m': state_mlstm_m,
            'c_ctx': c_ctx, 'ada_w': ada_w, 'ada_b': ada_b,
            'norm_mix_w': norm_mix_w, 'norm_ffn_w': norm_ffn_w,
            'ffn_w_gate': ffn_w_gate, 'ffn_w_up': ffn_w_up, 'ffn_w_down': ffn_w_down,
            'ssd_w_in': ssd_w_in, 'ssd_conv_w': ssd_conv_w, 'ssd_conv_b': ssd_conv_b,
            'ssd_dt_bias': ssd_dt_bias, 'ssd_a_log': ssd_a_log, 'ssd_d': ssd_d,
            'ssd_norm_w': ssd_norm_w, 'ssd_w_out': ssd_w_out,
            'attn_w_qkv': attn_w_qkv, 'attn_q_norm': attn_q_norm, 'attn_k_norm': attn_k_norm,
            'attn_w_out': attn_w_out,
            'mlstm_w_in': mlstm_w_in, 'mlstm_b_gates': mlstm_b_gates,
            'mlstm_head_norm': mlstm_head_norm, 'mlstm_w_out': mlstm_w_out}


def reference(x_prompt, x_sample, c, state_ssd, cache_attn_k, cache_attn_v, state_mlstm_C, state_mlstm_n,
              state_mlstm_m, c_ctx, ada_w, ada_b, norm_mix_w, norm_ffn_w, ffn_w_gate, ffn_w_up, ffn_w_down,
              ssd_w_in, ssd_conv_w, ssd_conv_b, ssd_dt_bias, ssd_a_log, ssd_d, ssd_norm_w, ssd_w_out,
              attn_w_qkv, attn_q_norm, attn_k_norm, attn_w_out,
              mlstm_w_in, mlstm_b_gates, mlstm_head_norm, mlstm_w_out):
    rows = x_sample.shape[1] // GRID_W
    bp = x_prompt.shape[0]
    xp, xs = x_prompt, x_sample
    new_ssd, new_k, new_v, new_C, new_n, new_m = [], [], [], [], [], []
    for l in range(DEPTH):
        kind, j = l % N_MIXERS, l // N_MIXERS
        mod_p = (jax.nn.silu(c_ctx) @ ada_w[l] + ada_b[l])[None, None, :]
        mod_s = (jax.nn.silu(c) @ ada_w[l] + ada_b[l])[:, None, :]
        sh1p, sc1p, g1p, sh2p, sc2p, g2p = jnp.split(mod_p, 6, axis=-1)
        sh1s, sc1s, g1s, sh2s, sc2s, g2s = jnp.split(mod_s, 6, axis=-1)
        up = modulate(rms_norm(xp, norm_mix_w[l]), sh1p, sc1p)
        us = modulate(rms_norm(xs, norm_mix_w[l]), sh1s, sc1s)
        if kind == 0:
            prm = (ssd_w_in[j], ssd_conv_w[j], ssd_conv_b[j], ssd_dt_bias[j], ssd_a_log[j], ssd_d[j],
                   ssd_norm_w[j], ssd_w_out[j])
            h_zero = jnp.zeros((bp, 2, SSD_HEADS, SSD_HEAD_DIM, SSD_D_STATE), xp.dtype)
            mp, st = ssd_mixer(up, h_zero, *prm)
            ms, _ = ssd_mixer(us, state_ssd[:, j], *prm)
            new_ssd.append(st)
        elif kind == 1:
            qp, kp, vp = attn_project(up, attn_w_qkv[j], attn_q_norm[j], attn_k_norm[j])
            mp = blocked_attention(qp, kp, vp) @ attn_w_out[j]
            qs, kl, vl = attn_project(us, attn_w_qkv[j], attn_q_norm[j], attn_k_norm[j])
            qs = axial_rope(qs, rows)
            kl = axial_rope(kl, rows)
            k_all = jnp.concatenate([cache_attn_k[:, j], kl], axis=1)
            v_all = jnp.concatenate([cache_attn_v[:, j], vl], axis=1)
            ms = blocked_attention(qs, k_all, v_all) @ attn_w_out[j]
            new_k.append(kp)
            new_v.append(vp)
        else:
            prm = (mlstm_w_in[j], mlstm_b_gates[j], mlstm_head_norm[j], mlstm_w_out[j])
            C_zero = jnp.zeros((bp, 2, MLSTM_HEADS, MLSTM_DK, MLSTM_DV), xp.dtype)
            n_zero = jnp.zeros((bp, 2, MLSTM_HEADS, MLSTM_DK), xp.dtype)
            m_zero = jnp.zeros((bp, 2, MLSTM_HEADS), xp.dtype)
            mp, Cc, nc_, mc = mlstm_mixer(up, C_zero, n_zero, m_zero, *prm)
            ms, _, _, _ = mlstm_mixer(us, state_mlstm_C[:, j], state_mlstm_n[:, j], state_mlstm_m[:, j], *prm)
            new_C.append(Cc)
            new_n.append(nc_)
            new_m.append(mc)
        xp = xp + g1p * mp
        xs = xs + g1s * ms
        xp = xp + g2p * swiglu(modulate(rms_norm(xp, norm_ffn_w[l]), sh2p, sc2p),
                               ffn_w_gate[l], ffn_w_up[l], ffn_w_down[l])
        xs = xs + g2s * swiglu(modulate(rms_norm(xs, norm_ffn_w[l]), sh2s, sc2s),
                               ffn_w_gate[l], ffn_w_up[l], ffn_w_down[l])
    return (xp, xs, jnp.stack(new_ssd, axis=1), jnp.stack(new_k, axis=1), jnp.stack(new_v, axis=1),
            jnp.stack(new_C, axis=1), jnp.stack(new_n, axis=1), jnp.stack(new_m, axis=1))
```

```python
import functools

import jax
import jax.numpy as jnp
from jax import lax
from jax.experimental import pallas as pl
from jax.experimental.pallas import tpu as pltpu

F32 = jnp.float32
BF16 = jnp.bfloat16

EPS = 1e-6
CHUNK = 128
GRID_W = 64
ROPE_THETA = 10000.0
SSD_HEAD_DIM = 64
SSD_D_STATE = 128
SSD_CONV_W = 3
SSD_HEADS_PER_GROUP = 8
ATTN_HEAD_DIM = 128
LANES = 128
SUBLANES = 8
VMEM_LIMIT_BYTES = 56 * 1024 * 1024

NT_DIMS = (((1,), (1,)), ((), ()))


def _silu(x):
    return x / (1.0 + jnp.exp(-x))


def _sigmoid(x):
    return 1.0 / (1.0 + jnp.exp(-x))


def _softplus(x):
    return jnp.maximum(x, 0.0) + jnp.log1p(jnp.exp(-jnp.abs(x)))


def _log_sigmoid(x):
    return -_softplus(-x)


def _rms(x, w):
    return x * lax.rsqrt(jnp.mean(x * x, axis=-1, keepdims=True) + EPS) * w


def _params(*sem):
    return pltpu.CompilerParams(dimension_semantics=sem, vmem_limit_bytes=VMEM_LIMIT_BYTES)


def _pick_tile(n, candidates):
    for c in candidates:
        if n % c == 0:
            return c
    return n


def _cumsum_chunk(a, axis, reverse):
    shape = [1, 1]
    shape[axis] = CHUNK
    idx = lax.broadcasted_iota(jnp.int32, tuple(shape), axis)
    sh = 1
    while sh < CHUNK:
        if reverse:
            a = a + jnp.where(idx < CHUNK - sh, pltpu.roll(a, CHUNK - sh, axis), 0.0)
        else:
            a = a + jnp.where(idx >= sh, pltpu.roll(a, sh, axis), 0.0)
        sh *= 2
    return a


def _mod_kernel(c_ref, w_ref, b_ref, o_ref):
    s = _silu(c_ref[...]).astype(BF16)
    o_ref[...] = jnp.dot(s, w_ref[...].astype(BF16), preferred_element_type=F32) + b_ref[...]


def _modulation(cc, ada_w, ada_b):
    depth, d, n = ada_w.shape
    rows = cc.shape[0]
    tn = _pick_tile(n, (1024, 512, 256, 128))
    return pl.pallas_call(
        _mod_kernel,
        out_shape=jax.ShapeDtypeStruct((depth, rows, n), F32),
        grid=(depth, n // tn),
        in_specs=[
            pl.BlockSpec((rows, d), lambda l, j: (0, 0)),
            pl.BlockSpec((None, d, tn), lambda l, j: (l, 0, j)),
            pl.BlockSpec((None, 1, tn), lambda l, j: (l, 0, j)),
        ],
        out_specs=pl.BlockSpec((None, rows, tn), lambda l, j: (l, 0, j)),
        compiler_params=_params("parallel", "parallel"),
        name="adaln_mod",
    )(cc, ada_w, ada_b.reshape(depth, 1, n))


class _Tokens:
    def __init__(self, n_prompt_tok, dec_seq):
        self.n_prompt_tok = n_prompt_tok
        self.dec_seq = dec_seq

    def row(self, start):
        return jnp.where(start < self.n_prompt_tok, 0, 1 + (start - self.n_prompt_tok) // self.dec_seq)


def _mod_spec(tok, tm, layer, k, d, grid_pos):
    def index_map(*ids):
        return (layer, tok.row(ids[grid_pos] * tm), k, 0, 0)
    return pl.BlockSpec((None, None, None, 1, d), index_map)


def _norm_mod(x_ref, nw_ref, sh_ref, sc_ref):
    return (_rms(x_ref[...], nw_ref[...]) * (1.0 + sc_ref[...]) + sh_ref[...]).astype(BF16)


def _nmm_kernel(x_ref, nw_ref, sh_ref, sc_ref, w_ref, o_ref, u_ref):
    @pl.when(pl.program_id(1) == 0)
    def _():
        u_ref[...] = _norm_mod(x_ref, nw_ref, sh_ref, sc_ref)

    o_ref[...] = jnp.dot(u_ref[...], w_ref[...], preferred_element_type=F32)


def _norm_mod_matmul(x, norm_w, mod5, layer, tok, w):
    t, d = x.shape
    n = w.shape[1]
    tm = _pick_tile(tok.n_prompt_tok, (512, 256, 128))
    tn = _pick_tile(n, (1152, 1024, 896, 768, 512, 384, 256, 128))
    return pl.pallas_call(
        _nmm_kernel,
        out_shape=jax.ShapeDtypeStruct((t, n), F32),
        grid=(t // tm, n // tn),
        in_specs=[
            pl.BlockSpec((tm, d), lambda i, j: (i, 0)),
            pl.BlockSpec((1, d), lambda i, j: (0, 0)),
            _mod_spec(tok, tm, layer, 0, d, 0),
            _mod_spec(tok, tm, layer, 1, d, 0),
            pl.BlockSpec((d, tn), lambda i, j: (0, j)),
        ],
        out_specs=pl.BlockSpec((tm, tn), lambda i, j: (i, j)),
        scratch_shapes=[pltpu.VMEM((tm, d), BF16)],
        compiler_params=_params("parallel", "arbitrary"),
        name="norm_mod_matmul",
    )(x, norm_w.reshape(1, d), mod5, mod5, w)


def _mmres_kernel(y_ref, w_ref, x_ref, g_ref, o_ref):
    o_ref[...] = x_ref[...] + g_ref[...] * jnp.dot(y_ref[...], w_ref[...], preferred_element_type=F32)


def _matmul_residual(y, w, x, mod5, layer, tok):
    t, kdim = y.shape
    d = x.shape[1]
    tm = _pick_tile(tok.n_prompt_tok, (512, 256, 128))
    tn = _pick_tile(d, (1024, 512, 256, 128))
    nj = d // tn

    def gate_map(j, i):
        return (layer, tok.row(i * tm), 2, 0, j)

    return pl.pallas_call(
        _mmres_kernel,
        out_shape=jax.ShapeDtypeStruct((t, d), F32),
        grid=(nj, t // tm),
        in_specs=[
            pl.BlockSpec((tm, kdim), lambda j, i: (i, 0)),
            pl.BlockSpec((kdim, tn), lambda j, i: (0, j)),
            pl.BlockSpec((tm, tn), lambda j, i: (i, j)),
            pl.BlockSpec((None, None, None, 1, tn), gate_map),
        ],
        out_specs=pl.BlockSpec((tm, tn), lambda j, i: (i, j)),
        compiler_params=_params("parallel", "parallel"),
        name="matmul_residual",
    )(y, w, x, mod5)


def _ffn_kernel(x_ref, nw_ref, sh_ref, sc_ref, g_ref, wg_ref, wu_ref, wd_ref, o_ref, u_ref, acc_ref):
    j = pl.program_id(1)

    @pl.when(j == 0)
    def _():
        u_ref[...] = _norm_mod(x_ref, nw_ref, sh_ref, sc_ref)
        acc_ref[...] = jnp.zeros_like(acc_ref)

    u = u_ref[...]
    a = jnp.dot(u, wg_ref[...], preferred_element_type=F32)
    b = jnp.dot(u, wu_ref[...], preferred_element_type=F32)
    h = (_silu(a) * b).astype(BF16)
    acc_ref[...] += jnp.dot(h, wd_ref[...], preferred_element_type=F32)

    @pl.when(j == pl.num_programs(1) - 1)
    def _():
        o_ref[...] = x_ref[...] + g_ref[...] * acc_ref[...]


def _ffn(x, norm_w, mod5, layer, tok, wg, wu, wd):
    t, d = x.shape
    hid = wg.shape[1]
    tm = _pick_tile(tok.n_prompt_tok, (512, 256, 128))
    th = _pick_tile(hid, (512, 256, 128))
    return pl.pallas_call(
        _ffn_kernel,
        out_shape=jax.ShapeDtypeStruct((t, d), F32),
        grid=(t // tm, hid // th),
        in_specs=[
            pl.BlockSpec((tm, d), lambda i, j: (i, 0)),
            pl.BlockSpec((1, d), lambda i, j: (0, 0)),
            _mod_spec(tok, tm, layer, 3, d, 0),
            _mod_spec(tok, tm, layer, 4, d, 0),
            _mod_spec(tok, tm, layer, 5, d, 0),
            pl.BlockSpec((d, th), lambda i, j: (0, j)),
            pl.BlockSpec((d, th), lambda i, j: (0, j)),
            pl.BlockSpec((th, d), lambda i, j: (j, 0)),
        ],
        out_specs=pl.BlockSpec((tm, d), lambda i, j: (i, 0)),
        scratch_shapes=[pltpu.VMEM((tm, d), BF16), pltpu.VMEM((tm, d), F32)],
        compiler_params=_params("parallel", "arbitrary"),
        name="ffn",
    )(x, norm_w.reshape(1, d), mod5, mod5, mod5, wg, wu, wd)


def _ssd_kernel(nc, has_h0, has_ht, *refs):
    (z_ref, x_ref, b_ref, c_ref, dtc_ref, dtr_ref, pc_ref, pr_ref, dsk_ref, nw_ref,
     cwx_ref, cwb_ref, cwc_ref, cbx_ref, cbb_ref, cbc_ref) = refs[:16]
    pos = 16
    h0_ref = None
    ht_ref = None
    if has_h0:
        h0_ref = refs[pos]
        pos += 1
    y_ref = refs[pos]
    pos += 1
    if has_ht:
        ht_ref = refs[pos]
        pos += 1
    xs_scr, bs_scr, cs_scr, yb_scr, h_scr = refs[pos:]

    seq_len = nc * CHUNK
    pairs = x_ref.shape[1] // LANES
    row = lax.broadcasted_iota(jnp.int32, (CHUNK, 1), 0)
    lo = lax.broadcasted_iota(jnp.int32, (1, LANES), 1) < SSD_HEAD_DIM
    ti = lax.broadcasted_iota(jnp.int32, (CHUNK, CHUNK), 0)
    si = lax.broadcasted_iota(jnp.int32, (CHUNK, CHUNK), 1)

    def conv_chunk(c, carry):
        r0 = pl.multiple_of(c * CHUNK, CHUNK)
        rp = pl.multiple_of(jnp.maximum(r0 - SUBLANES, 0), SUBLANES)
        rn = pl.multiple_of(jnp.minimum(r0 + CHUNK, seq_len - SUBLANES), SUBLANES)
        for src, w_ref, bias_ref, dst in ((x_ref, cwx_ref, cbx_ref, xs_scr),
                                          (b_ref, cwb_ref, cbb_ref, bs_scr),
                                          (c_ref, cwc_ref, cbc_ref, cs_scr)):
            cur = src[pl.ds(r0, CHUNK), :]
            prow = jnp.where(c > 0, src[pl.ds(rp, SUBLANES), :][SUBLANES - 1:SUBLANES, :], 0.0)
            nrow = jnp.where(c < nc - 1, src[pl.ds(rn, SUBLANES), :][0:1, :], 0.0)
            xm1 = jnp.where(row == 0, prow, pltpu.roll(cur, 1, 0))
            xp1 = jnp.where(row == CHUNK - 1, nrow, pltpu.roll(cur, CHUNK - 1, 0))
            w = w_ref[...]
            v = w[0:1, :] * xm1 + w[1:2, :] * cur + w[2:3, :] * xp1 + bias_ref[...]
            dst[pl.ds(r0, CHUNK), :] = _silu(v).astype(dst.dtype)
        return carry

    def chunk_step(c, d):
        r0 = pl.multiple_of(c * CHUNK, CHUNK)
        xc = xs_scr[pl.ds(r0, CHUNK), :]
        bc = bs_scr[pl.ds(r0, CHUNK), :]
        cc = cs_scr[pl.ds(r0, CHUNK), :]
        pc = pc_ref[...]
        pr = pr_ref[...]
        dt_c = _softplus(dtc_ref[pl.ds(r0, CHUNK), :] + pc[0:1, :])
        acs_c = _cumsum_chunk(dt_c * (-jnp.exp(pc[1:2, :])), 0, d == 1)
        dt_r = _softplus(dtr_ref[:, pl.ds(r0, CHUNK)] + pr[:, 0:1])
        acs_r = _cumsum_chunk(dt_r * (-jnp.exp(pr[:, 1:2])), 1, d == 1)
        e_c = jnp.exp(acs_c)
        end = acs_c[CHUNK - 1:CHUNK, :] if d == 0 else acs_c[0:1, :]
        dec_c = jnp.exp(end - acs_c)
        tot = jnp.exp(end)
        cb = lax.dot_general(cc, bc, NT_DIMS, preferred_element_type=F32)
        mask = (ti >= si) if d == 0 else (ti <= si)

        def decay_cb(k):
            seg = acs_c[:, k:k + 1] - acs_r[k:k + 1, :]
            return (cb * jnp.exp(jnp.where(mask, seg, -jnp.inf))).astype(BF16)

        for p in range(pairs):
            k0 = SSD_HEADS_PER_GROUP * d + 2 * p
            k1 = k0 + 1
            cols = slice(LANES * p, LANES * (p + 1))
            xp = xc[:, cols]
            xdt = xp * jnp.where(lo, dt_c[:, k0:k0 + 1], dt_c[:, k1:k1 + 1])
            xdt_b = xdt.astype(BF16)
            zero = jnp.zeros_like(xdt_b)
            y = jnp.dot(decay_cb(k0), jnp.where(lo, xdt_b, zero), preferred_element_type=F32)
            y += jnp.dot(decay_cb(k1), jnp.where(lo, zero, xdt_b), preferred_element_type=F32)
            hp = h_scr[cols, :]
            y_inter = lax.dot_general(cc, hp.astype(BF16), NT_DIMS, preferred_element_type=F32)
            y += y_inter * jnp.where(lo, e_c[:, k0:k0 + 1], e_c[:, k1:k1 + 1])
            xdtd = xdt * jnp.where(lo, dec_c[:, k0:k0 + 1], dec_c[:, k1:k1 + 1])
            s_new = jnp.dot(xdtd.T.astype(BF16), bc, preferred_element_type=F32)
            tot_p = jnp.where(row < SSD_HEAD_DIM, tot[:, k0:k0 + 1], tot[:, k1:k1 + 1])
            h_scr[cols, :] = tot_p * hp + s_new
            if d == 1:
                yb_scr[pl.ds(r0, CHUNK), cols] = y
            else:
                yb_scr[pl.ds(r0, CHUNK), cols] = y + yb_scr[pl.ds(r0, CHUNK), cols] + dsk_ref[:, cols] * xp

        if d == 0:
            g = yb_scr[pl.ds(r0, CHUNK), :] * _silu(z_ref[pl.ds(r0, CHUNK), :])
            y_ref[pl.ds(r0, CHUNK), :] = _rms(g, nw_ref[...]).astype(y_ref.dtype)

    def run_direction(d):
        if has_h0:
            h_scr[...] = h0_ref[d]
        else:
            h_scr[...] = jnp.zeros_like(h_scr)

        def body(i, carry):
            chunk_step(i if d == 0 else nc - 1 - i, d)
            return carry

        lax.fori_loop(0, nc, body, 0)
        if has_ht:
            ht_ref[d] = h_scr[...]

    lax.fori_loop(0, nc, conv_chunk, 0)
    run_direction(1)
    run_direction(0)


def _ssd_scan(proj, dtg, dtg_t, pc, pr, dsk, norm_w, conv_w, conv_b, row0, nseq, seq_len, h0, want_state):
    groups = dtg.shape[0]
    gw = SSD_HEADS_PER_GROUP * SSD_HEAD_DIM
    d_inner = groups * gw
    n = SSD_D_STATE
    nc = seq_len // CHUNK
    rb0 = row0 // seq_len
    xoff = d_inner // gw
    boff = 2 * d_inner // n
    coff = boff + groups
    cwx_off = 0
    cwb_off = d_inner // n
    cwc_off = cwb_off + groups

    in_specs = [
        pl.BlockSpec((seq_len, gw), lambda b, g: (rb0 + b, g)),
        pl.BlockSpec((seq_len, gw), lambda b, g: (rb0 + b, xoff + g)),
        pl.BlockSpec((seq_len, n), lambda b, g: (rb0 + b, boff + g)),
        pl.BlockSpec((seq_len, n), lambda b, g: (rb0 + b, coff + g)),
        pl.BlockSpec((None, seq_len, 2 * SSD_HEADS_PER_GROUP), lambda b, g: (g, rb0 + b, 0)),
        pl.BlockSpec((None, 2 * SSD_HEADS_PER_GROUP, seq_len), lambda b, g: (g, 0, rb0 + b)),
        pl.BlockSpec((None, 2, 2 * SSD_HEADS_PER_GROUP), lambda b, g: (g, 0, 0)),
        pl.BlockSpec((None, 2 * SSD_HEADS_PER_GROUP, 2), lambda b, g: (g, 0, 0)),
        pl.BlockSpec((1, gw), lambda b, g: (0, g)),
        pl.BlockSpec((1, gw), lambda b, g: (0, g)),
        pl.BlockSpec((SSD_CONV_W, gw), lambda b, g: (0, cwx_off + g)),
        pl.BlockSpec((SSD_CONV_W, n), lambda b, g: (0, cwb_off + g)),
        pl.BlockSpec((SSD_CONV_W, n), lambda b, g: (0, cwc_off + g)),
        pl.BlockSpec((1, gw), lambda b, g: (0, cwx_off + g)),
        pl.BlockSpec((1, n), lambda b, g: (0, cwb_off + g)),
        pl.BlockSpec((1, n), lambda b, g: (0, cwc_off + g)),
    ]
    args = [proj, proj, proj, proj, dtg, dtg_t, pc, pr, dsk, norm_w,
            conv_w, conv_w, conv_w, conv_b, conv_b, conv_b]
    if h0 is not None:
        in_specs.append(pl.BlockSpec((None, 2, gw, n), lambda b, g: (b, 0, g, 0)))
        args.append(h0)
    out_shape = [jax.ShapeDtypeStruct((nseq * seq_len, d_inner), BF16)]
    out_specs = [pl.BlockSpec((seq_len, gw), lambda b, g: (b, g))]
    if want_state:
        out_shape.append(jax.ShapeDtypeStruct((nseq, 2, d_inner, n), F32))
        out_specs.append(pl.BlockSpec((None, 2, gw, n), lambda b, g: (b, 0, g, 0)))
    outs = pl.pallas_call(
        functools.partial(_ssd_kernel, nc, h0 is not None, want_state),
        out_shape=out_shape,
        grid=(nseq, groups),
        in_specs=in_specs,
        out_specs=out_specs,
        scratch_shapes=[
            pltpu.VMEM((seq_len, gw), F32),
            pltpu.VMEM((seq_len, n), BF16),
            pltpu.VMEM((seq_len, n), BF16),
            pltpu.VMEM((seq_len, gw), F32),
            pltpu.VMEM((gw, n), F32),
        ],
        compiler_params=_params("parallel", "parallel"),
        name="ssd_scan",
    )(*args)
    return outs if want_state else (outs[0], None)


def _ssd_mixer(x, norm_w_mix, mod5, layer, tok, geo, h0_sample, w_in, conv_w, conv_b, dt_bias, a_log,
               d_skip, norm_w, w_out):
    heads = dt_bias.shape[1]
    groups = heads // SSD_HEADS_PER_GROUP
    d_inner = heads * SSD_HEAD_DIM
    hpg = SSD_HEADS_PER_GROUP
    t = x.shape[0]
    proj = _norm_mod_matmul(x, norm_w_mix, mod5, layer, tok, w_in.astype(BF16))
    dt_off = d_inner + conv_w.shape[1]
    dt_raw = proj[:, dt_off:dt_off + 2 * heads].reshape(t, 2, groups, hpg)
    dtg = dt_raw.transpose(2, 0, 1, 3).reshape(groups, t, 2 * hpg)
    dtg_t = dtg.transpose(0, 2, 1)

    def per_group(v):
        return v.reshape(2, groups, hpg).transpose(1, 0, 2).reshape(groups, 2 * hpg)

    pc = jnp.stack([per_group(dt_bias), per_group(a_log)], axis=1)
    pr = pc.transpose(0, 2, 1)
    dsk = jnp.repeat(d_skip, SSD_HEAD_DIM).reshape(1, d_inner)
    nw = norm_w.reshape(1, d_inner)
    cb = conv_b.reshape(1, -1)
    bp, lp, bs, ls = geo
    y_p, st = _ssd_scan(proj, dtg, dtg_t, pc, pr, dsk, nw, conv_w, cb, 0, bp, lp, None, True)
    h0 = h0_sample.reshape(bs, 2, d_inner, SSD_D_STATE)
    y_s, _ = _ssd_scan(proj, dtg, dtg_t, pc, pr, dsk, nw, conv_w, cb, bp * lp, bs, ls, h0, False)
    y = jnp.concatenate([y_p, y_s], axis=0)
    x = _matmul_residual(y, w_out.astype(BF16), x, mod5, layer, tok)
    return x, st.reshape(bp, 2, heads, SSD_HEAD_DIM, SSD_D_STATE)


def _rope(x, cos, sin):
    quarter = ATTN_HEAD_DIM // 4
    lane = lax.broadcasted_iota(jnp.int32, (1, ATTN_HEAD_DIM), 1)
    first = (lane % (2 * quarter)) < quarter
    swapped = jnp.where(first, pltpu.roll(x, ATTN_HEAD_DIM - quarter, 1), pltpu.roll(x, quarter, 1))
    return x * cos + swapped * sin


def _attn_kernel(nq, has_cache, use_rope, write_k, *refs):
    q_ref, k_ref, v_ref, qn_ref, kn_ref = refs[:5]
    pos = 5
    cos_ref = sin_ref = kc_ref = vc_ref = knew_ref = None
    if use_rope:
        cos_ref, sin_ref = refs[pos:pos + 2]
        pos += 2
    if has_cache:
        kc_ref, vc_ref = refs[pos:pos + 2]
        pos += 2
    o_ref = refs[pos]
    pos += 1
    if write_k:
        knew_ref = refs[pos]
        pos += 1
    kb_scr, vb_scr = refs[pos:pos + 2]
    pos += 2
    if has_cache:
        kcb_scr, vcb_scr = refs[pos:pos + 2]
        kcb_scr[...] = kc_ref[...].astype(BF16)
        vcb_scr[...] = vc_ref[...].astype(BF16)

    grp = q_ref.shape[1] // ATTN_HEAD_DIM
    scale = ATTN_HEAD_DIM ** -0.5

    def kprep(c, carry):
        r0 = pl.multiple_of(c * CHUNK, CHUNK)
        kn = _rms(k_ref[pl.ds(r0, CHUNK), :], kn_ref[...])
        if write_k:
            knew_ref[pl.ds(r0, CHUNK), :] = kn
        if use_rope:
            kn = _rope(kn, cos_ref[pl.ds(r0, CHUNK), :], sin_ref[pl.ds(r0, CHUNK), :])
        kb_scr[pl.ds(r0, CHUNK), :] = kn.astype(BF16)
        vb_scr[pl.ds(r0, CHUNK), :] = v_ref[pl.ds(r0, CHUNK), :].astype(BF16)
        return carry

    lax.fori_loop(0, nq, kprep, 0)

    def qblock(c, carry):
        r0 = pl.multiple_of(c * CHUNK, CHUNK)
        for g in range(grp):
            cols = slice(ATTN_HEAD_DIM * g, ATTN_HEAD_DIM * (g + 1))
            qn = _rms(q_ref[pl.ds(r0, CHUNK), cols], qn_ref[...])
            if use_rope:
                qn = _rope(qn, cos_ref[pl.ds(r0, CHUNK), :], sin_ref[pl.ds(r0, CHUNK), :])
            qb = qn.astype(BF16)
            s = lax.dot_general(qb, kb_scr[...], NT_DIMS, preferred_element_type=F32) * scale
            m = jnp.max(s, axis=1, keepdims=True)
            if has_cache:
                sc = lax.dot_general(qb, kcb_scr[...], NT_DIMS, preferred_element_type=F32) * scale
                m = jnp.maximum(m, jnp.max(sc, axis=1, keepdims=True))
                ec = jnp.exp(sc - m)
            e = jnp.exp(s - m)
            den = jnp.sum(e, axis=1, keepdims=True)
            if has_cache:
                den = den + jnp.sum(ec, axis=1, keepdims=True)
            o = jnp.dot((e / den).astype(BF16), vb_scr[...], preferred_element_type=F32)
            if has_cache:
                o += jnp.dot((ec / den).astype(BF16), vcb_scr[...], preferred_element_type=F32)
            o_ref[pl.ds(r0, CHUNK), cols] = o.astype(o_ref.dtype)
        return carry

    lax.fori_loop(0, nq, qblock, 0)


def _attention(qkv, q_norm, k_norm, heads, kv_heads, row0, nseq, seq_len, rope, cache, write_k):
    dh = ATTN_HEAD_DIM
    grp = heads // kv_heads
    rb0 = row0 // seq_len
    nq = seq_len // CHUNK
    in_specs = [
        pl.BlockSpec((seq_len, grp * dh), lambda b, h: (rb0 + b, h)),
        pl.BlockSpec((seq_len, dh), lambda b, h: (rb0 + b, heads + h)),
        pl.BlockSpec((seq_len, dh), lambda b, h: (rb0 + b, heads + kv_heads + h)),
        pl.BlockSpec((1, dh), lambda b, h: (0, 0)),
        pl.BlockSpec((1, dh), lambda b, h: (0, 0)),
    ]
    args = [qkv, qkv, qkv, q_norm.reshape(1, dh), k_norm.reshape(1, dh)]
    scratch = [pltpu.VMEM((seq_len, dh), BF16), pltpu.VMEM((seq_len, dh), BF16)]
    if rope is not None:
        in_specs += [pl.BlockSpec((seq_len, dh), lambda b, h: (0, 0))] * 2
        args += list(rope)
    if cache is not None:
        past = cache[0].shape[1]
        in_specs += [pl.BlockSpec((None, past, dh), lambda b, h: (b, 0, h))] * 2
        args += list(cache)
        scratch += [pltpu.VMEM((past, dh), BF16), pltpu.VMEM((past, dh), BF16)]
    out_shape = [jax.ShapeDtypeStruct((nseq * seq_len, heads * dh), BF16)]
    out_specs = [pl.BlockSpec((seq_len, grp * dh), lambda b, h: (b, h))]
    if write_k:
        out_shape.append(jax.ShapeDtypeStruct((nseq * seq_len, kv_heads * dh), F32))
        out_specs.append(pl.BlockSpec((seq_len, dh), lambda b, h: (b, h)))
    outs = pl.pallas_call(
        functools.partial(_attn_kernel, nq, cache is not None, rope is not None, write_k),
        out_shape=out_shape,
        grid=(nseq, kv_heads),
        in_specs=in_specs,
        out_specs=out_specs,
        scratch_shapes=scratch,
        compiler_params=_params("parallel", "parallel"),
        name="attention",
    )(*args)
    return outs if write_k else (outs[0], None)


def _rope_tables(seq_len):
    quarter = ATTN_HEAD_DIM // 4
    t = jnp.arange(seq_len)
    lane = jnp.arange(ATTN_HEAD_DIM)
    pos = jnp.where(lane[None, :] < 2 * quarter, (t // GRID_W)[:, None], (t % GRID_W)[:, None]).astype(F32)
    inv_freq = ROPE_THETA ** (-jnp.arange(quarter, dtype=F32) / quarter)
    ang = pos * inv_freq[lane % quarter][None, :]
    sign = jnp.where((lane % (2 * quarter)) < quarter, -1.0, 1.0).astype(F32)
    return jnp.cos(ang), jnp.sin(ang) * sign[None, :]


def _attn_mixer(x, norm_w_mix, mod5, layer, tok, geo, cache_k, cache_v, w_qkv, q_norm, k_norm, w_out):
    bp, lp, bs, ls = geo
    dh = ATTN_HEAD_DIM
    kv_heads = cache_k.shape[2]
    heads = w_out.shape[0] // dh
    qkv = _norm_mod_matmul(x, norm_w_mix, mod5, layer, tok, w_qkv.astype(BF16))
    o_p, k_new = _attention(qkv, q_norm, k_norm, heads, kv_heads, 0, bp, lp, None, None, True)
    past = cache_k.shape[1]
    cache = (cache_k.reshape(bs, past, kv_heads * dh), cache_v.reshape(bs, past, kv_heads * dh))
    o_s, _ = _attention(qkv, q_norm, k_norm, heads, kv_heads, bp * lp, bs, ls, _rope_tables(ls), cache, False)
    o = jnp.concatenate([o_p, o_s], axis=0)
    x = _matmul_residual(o, w_out.astype(BF16), x, mod5, layer, tok)
    v_new = qkv[:bp * lp, (heads + kv_heads) * dh:]
    return x, k_new.reshape(bp, lp, kv_heads, dh), v_new.reshape(bp, lp, kv_heads, dh)


def _mlstm_kernel(nc, has_s0, has_st, *refs):
    q_ref, k_ref, v_ref, og_ref, gc_ref, gr_ref, bgc_ref, bgr_ref, hn_ref = refs[:9]
    pos = 9
    c0_ref = n0_ref = m0_ref = ct_ref = nt_ref = mt_ref = None
    if has_s0:
        c0_ref, n0_ref, m0_ref = refs[pos:pos + 3]
        pos += 3
    h_ref = refs[pos]
    pos += 1
    if has_st:
        ct_ref, nt_ref, mt_ref = refs[pos:pos + 3]
        pos += 3
    hb_scr, c_scr, n_scr, m_scr = refs[pos:]

    dk = q_ref.shape[1]
    qscale = dk ** -0.5
    ti = lax.broadcasted_iota(jnp.int32, (CHUNK, CHUNK), 0)
    si = lax.broadcasted_iota(jnp.int32, (CHUNK, CHUNK), 1)

    def chunk_step(c, d):
        r0 = pl.multiple_of(c * CHUNK, CHUNK)
        q = q_ref[pl.ds(r0, CHUNK), :] * qscale
        k = k_ref[pl.ds(r0, CHUNK), :]
        qb = q.astype(BF16)
        kb = k.astype(BF16)
        vb = v_ref[pl.ds(r0, CHUNK), :].astype(BF16)
        gc = gc_ref[pl.ds(r0, CHUNK), :] + bgc_ref[...]
        gr = gr_ref[:, pl.ds(r0, CHUNK)] + bgr_ref[...]
        i_c = gc[:, 2 * d:2 * d + 1]
        i_r = gr[2 * d:2 * d + 1, :]
        bc = _cumsum_chunk(_log_sigmoid(gc[:, 2 * d + 1:2 * d + 2]), 0, d == 1)
        br = _cumsum_chunk(_log_sigmoid(gr[2 * d + 1:2 * d + 2, :]), 1, d == 1)
        m_prev = m_scr[:, 0:1]
        mask = (ti >= si) if d == 0 else (ti <= si)
        dm = jnp.where(mask, bc - br + i_r, -jnp.inf)
        m_inter = bc + m_prev
        m_t = jnp.maximum(m_inter, jnp.max(dm, axis=1, keepdims=True))
        s = lax.dot_general(qb, kb, NT_DIMS, preferred_element_type=F32) * jnp.exp(dm - m_t)
        inter = jnp.exp(m_inter - m_t)
        cst = c_scr[...]
        nst = n_scr[...]
        num = jnp.dot(s.astype(BF16), vb, preferred_element_type=F32)
        num += inter * jnp.dot(qb, cst.astype(BF16), preferred_element_type=F32)
        den = jnp.sum(s, axis=1, keepdims=True) + inter * jnp.sum(q * nst, axis=1, keepdims=True)
        h = num / jnp.maximum(jnp.abs(den), jnp.exp(-m_t))
        b_end = bc[CHUNK - 1:CHUNK, :] if d == 0 else bc[0:1, :]
        log_w = b_end - bc + i_c
        m_new = jnp.maximum(b_end + m_prev, jnp.max(log_w, axis=0, keepdims=True))
        wk = jnp.exp(log_w - m_new) * k
        carry_decay = jnp.exp(b_end + m_prev - m_new)
        c_scr[...] = carry_decay * cst + jnp.dot(wk.T.astype(BF16), vb, preferred_element_type=F32)
        n_scr[...] = carry_decay * nst + jnp.sum(wk, axis=0, keepdims=True)
        m_scr[...] = jnp.broadcast_to(m_new, m_scr.shape)
        if d == 1:
            hb_scr[pl.ds(r0, CHUNK), :] = h
        else:
            hs = _rms(h + hb_scr[pl.ds(r0, CHUNK), :], hn_ref[...])
            h_ref[pl.ds(r0, CHUNK), :] = (hs * _sigmoid(og_ref[pl.ds(r0, CHUNK), :])).astype(h_ref.dtype)

    def run_direction(d):
        if has_s0:
            c_scr[...] = c0_ref[d]
            n_scr[...] = n0_ref[d]
            m_scr[...] = m0_ref[d]
        else:
            c_scr[...] = jnp.zeros_like(c_scr)
            n_scr[...] = jnp.zeros_like(n_scr)
            m_scr[...] = jnp.zeros_like(m_scr)

        def body(i, carry):
            chunk_step(i if d == 0 else nc - 1 - i, d)
            return carry

        lax.fori_loop(0, nc, body, 0)
        if has_st:
            ct_ref[d] = c_scr[...]
            nt_ref[d] = n_scr[...]
            mt_ref[d] = m_scr[...]

    run_direction(1)
    run_direction(0)


def _mlstm_scan(proj, gates_c, gates_r, bg_c, bg_r, head_norm, heads, dk, dv, row0, nseq, seq_len, state, want_state):
    assert dv == 2 * dk
    nc = seq_len // CHUNK
    rb0 = row0 // seq_len
    in_specs = [
        pl.BlockSpec((seq_len, dk), lambda b, h: (rb0 + b, h)),
        pl.BlockSpec((seq_len, dk), lambda b, h: (rb0 + b, heads + h)),
        pl.BlockSpec((seq_len, dv), lambda b, h: (rb0 + b, heads + h)),
        pl.BlockSpec((seq_len, dv), lambda b, h: (rb0 + b, 2 * heads + h)),
        pl.BlockSpec((None, seq_len, 4), lambda b, h: (h, rb0 + b, 0)),
        pl.BlockSpec((None, 4, seq_len), lambda b, h: (h, 0, rb0 + b)),
        pl.BlockSpec((None, 1, 4), lambda b, h: (h, 0, 0)),
        pl.BlockSpec((None, 4, 1), lambda b, h: (h, 0, 0)),
        pl.BlockSpec((None, 1, dv), lambda b, h: (h, 0, 0)),
    ]
    args = [proj, proj, proj, proj, gates_c, gates_r, bg_c, bg_r, head_norm]
    state_specs = [
        pl.BlockSpec((None, 2, None, dk, dv), lambda b, h: (b, 0, h, 0, 0)),
        pl.BlockSpec((None, 2, None, 1, dk), lambda b, h: (b, 0, h, 0, 0)),
        pl.BlockSpec((None, 2, None, 1, LANES), lambda b, h: (b, 0, h, 0, 0)),
    ]
    if state is not None:
        in_specs += state_specs
        args += list(state)
    out_shape = [jax.ShapeDtypeStruct((nseq * seq_len, heads * dv), BF16)]
    out_specs = [pl.BlockSpec((seq_len, dv), lambda b, h: (b, h))]
    if want_state:
        out_shape += [jax.ShapeDtypeStruct((nseq, 2, heads, dk, dv), F32),
                      jax.ShapeDtypeStruct((nseq, 2, heads, 1, dk), F32),
                      jax.ShapeDtypeStruct((nseq, 2, heads, 1, LANES), F32)]
        out_specs += state_specs
    outs = pl.pallas_call(
        functools.partial(_mlstm_kernel, nc, state is not None, want_state),
        out_shape=out_shape,
        grid=(nseq, heads),
        in_specs=in_specs,
        out_specs=out_specs,
        scratch_shapes=[
            pltpu.VMEM((seq_len, dv), F32),
            pltpu.VMEM((dk, dv), F32),
            pltpu.VMEM((1, dk), F32),
            pltpu.VMEM((1, LANES), F32),
        ],
        compiler_params=_params("parallel", "parallel"),
        name="mlstm_scan",
    )(*args)
    return outs


def _mlstm_mixer(x, norm_w_mix, mod5, layer, tok, geo, c0, n0, m0, w_in, b_gates, head_norm, w_out):
    bp, lp, bs, ls = geo
    heads, dk, dv = c0.shape[2], c0.shape[3], c0.shape[4]
    t = x.shape[0]
    n_in = w_in.shape[1]
    n_pad = -(-n_in // LANES) * LANES
    w_pad = jnp.pad(w_in.astype(BF16), ((0, 0), (0, n_pad - n_in)))
    proj = _norm_mod_matmul(x, norm_w_mix, mod5, layer, tok, w_pad)
    g_off = 2 * heads * dk + 2 * heads * dv
    gates_c = proj[:, g_off:g_off + 4 * heads].reshape(t, 4, heads).transpose(2, 0, 1)
    gates_r = gates_c.transpose(0, 2, 1)
    bg_r = b_gates.reshape(4, heads).T.reshape(heads, 4, 1)
    bg_c = bg_r.reshape(heads, 1, 4)
    hn = head_norm.reshape(heads, 1, dv)
    outs_p = _mlstm_scan(proj, gates_c, gates_r, bg_c, bg_r, hn, heads, dk, dv, 0, bp, lp, None, True)
    h_p, c_t, n_t, m_t = outs_p
    state = (c0, n0.reshape(bs, 2, heads, 1, dk),
             jnp.broadcast_to(m0[..., None, None], (bs, 2, heads, 1, LANES)))
    (h_s,) = _mlstm_scan(proj, gates_c, gates_r, bg_c, bg_r, hn, heads, dk, dv, bp * lp, bs, ls, state, False)
    h = jnp.concatenate([h_p, h_s], axis=0)
    x = _matmul_residual(h, w_out.astype(BF16), x, mod5, layer, tok)
    return x, c_t, n_t.reshape(bp, 2, heads, dk), m_t[:, :, :, 0, 0]


def kernel(x_prompt, x_sample, c, state_ssd, cache_attn_k, cache_attn_v, state_mlstm_C, state_mlstm_n, state_mlstm_m, c_ctx, ada_w, ada_b, norm_mix_w, norm_ffn_w, ffn_w_gate, ffn_w_up, ffn_w_down, ssd_w_in, ssd_conv_w, ssd_conv_b, ssd_dt_bias, ssd_a_log, ssd_d, ssd_norm_w, ssd_w_out, attn_w_qkv, attn_q_norm, attn_k_norm, attn_w_out, mlstm_w_in, mlstm_b_gates, mlstm_head_norm, mlstm_w_out):
    bp, lp, d = x_prompt.shape
    bs, ls, _ = x_sample.shape
    depth = ada_w.shape[0]
    geo = (bp, lp, bs, ls)
    tok = _Tokens(bp * lp, ls)
    x = jnp.concatenate([x_prompt.reshape(bp * lp, d), x_sample.reshape(bs * ls, d)], axis=0)

    mod_rows = -(-(1 + bs) // SUBLANES) * SUBLANES
    cc = jnp.concatenate([c_ctx[None, :], c, jnp.zeros((mod_rows - 1 - bs, d), F32)], axis=0)
    mod5 = _modulation(cc, ada_w, ada_b).reshape(depth, mod_rows, 6, 1, d)

    new_ssd, new_k, new_v, new_c, new_n, new_m = [], [], [], [], [], []
    for l in range(depth):
        kind, j = l % 3, l // 3
        if kind == 0:
            x, st = _ssd_mixer(x, norm_mix_w[l], mod5, l, tok, geo, state_ssd[:, j], ssd_w_in[j], ssd_conv_w[j],
                               ssd_conv_b[j], ssd_dt_bias[j], ssd_a_log[j], ssd_d[j], ssd_norm_w[j], ssd_w_out[j])
            new_ssd.append(st)
        elif kind == 1:
            x, k_new, v_new = _attn_mixer(x, norm_mix_w[l], mod5, l, tok, geo, cache_attn_k[:, j], cache_attn_v[:, j],
                                          attn_w_qkv[j], attn_q_norm[j], attn_k_norm[j], attn_w_out[j])
            new_k.append(k_new)
            new_v.append(v_new)
        else:
            x, c_t, n_t, m_t = _mlstm_mixer(x, norm_mix_w[l], mod5, l, tok, geo, state_mlstm_C[:, j],
                                            state_mlstm_n[:, j], state_mlstm_m[:, j], mlstm_w_in[j],
                                            mlstm_b_gates[j], mlstm_head_norm[j], mlstm_w_out[j])
            new_c.append(c_t)
            new_n.append(n_t)
            new_m.append(m_t)
        x = _ffn(x, norm_ffn_w[l], mod5, l, tok, ffn_w_gate[l].astype(BF16), ffn_w_up[l].astype(BF16),
                 ffn_w_down[l].astype(BF16))

    y_prompt = x[:bp * lp].reshape(bp, lp, d)
    y_sample = x[bp * lp:].reshape(bs, ls, d)
    return (y_prompt, y_sample, jnp.stack(new_ssd, axis=1), jnp.stack(new_k, axis=1), jnp.stack(new_v, axis=1),
            jnp.stack(new_c, axis=1), jnp.stack(new_n, axis=1), jnp.stack(new_m, axis=1))
```

```python
import functools

import jax
import jax.numpy as jnp
from jax import lax
from jax.experimental import pallas as pl
from jax.experimental.pallas import tpu as pltpu

F32 = jnp.float32
BF16 = jnp.bfloat16

EPS = 1e-6
CHUNK = 128
GRID_W = 64
ROPE_THETA = 10000.0
SSD_HEAD_DIM = 64
SSD_D_STATE = 128
SSD_CONV_W = 3
SSD_HEADS_PER_GROUP = 8
ATTN_HEAD_DIM = 128
LANES = 128
SUBLANES = 8
VMEM_LIMIT_BYTES = 56 * 1024 * 1024

NT_DIMS = (((1,), (1,)), ((), ()))
LOG2_E = 1.4426950408889634


def _sigmoid(x):
    return 0.5 + 0.5 * jnp.tanh(0.5 * x)


def _silu(x):
    return x * _sigmoid(x)


def _softplus(x):
    return jnp.maximum(x, 0.0) + jnp.log1p(jnp.exp(-jnp.abs(x)))


def _log_sigmoid(x):
    return -_softplus(-x)


def _rms(x, w):
    return x * lax.rsqrt(jnp.mean(x * x, axis=-1, keepdims=True) + EPS) * w


def _params(*sem):
    return pltpu.CompilerParams(dimension_semantics=sem, vmem_limit_bytes=VMEM_LIMIT_BYTES)


def _pick_tile(n, candidates):
    for c in candidates:
        if n % c == 0:
            return c
    return n


def _scan_chunk(a, reverse, combine, identity):
    idx = lax.broadcasted_iota(jnp.int32, (CHUNK, 1), 0)
    sh = 1
    while sh < CHUNK:
        if reverse:
            a = combine(a, jnp.where(idx < CHUNK - sh, pltpu.roll(a, CHUNK - sh, 0), identity))
        else:
            a = combine(a, jnp.where(idx >= sh, pltpu.roll(a, sh, 0), identity))
        sh *= 2
    return a


def _cumsum_chunk(a, reverse):
    return _scan_chunk(a, reverse, jnp.add, 0.0)


def _cummax_chunk(a, reverse):
    return _scan_chunk(a, reverse, jnp.maximum, -jnp.inf)


def _cast_kernel(w_ref, o_ref):
    o_ref[...] = w_ref[...].astype(BF16)


def _to_bf16(w):
    shape = w.shape
    w2 = w.reshape(-1, shape[-1])
    rows, n = w2.shape
    tr = _pick_tile(rows, (256, 128, 64, 32, 16))
    out = pl.pallas_call(
        _cast_kernel,
        out_shape=jax.ShapeDtypeStruct((rows, n), BF16),
        grid=(rows // tr,),
        in_specs=[pl.BlockSpec((tr, n), lambda i: (i, 0))],
        out_specs=pl.BlockSpec((tr, n), lambda i: (i, 0)),
        compiler_params=_params("parallel"),
        name="cast_bf16",
    )(w2)
    return out.reshape(shape)


def _mod_kernel(c_ref, w_ref, b_ref, o_ref):
    s = _silu(c_ref[...]).astype(BF16)
    o_ref[...] = jnp.dot(s, w_ref[...].astype(BF16), preferred_element_type=F32) + b_ref[...]


def _modulation(cc, ada_w, ada_b):
    depth, d, n = ada_w.shape
    rows = cc.shape[0]
    tn = _pick_tile(n, (1024, 512, 256, 128))
    return pl.pallas_call(
        _mod_kernel,
        out_shape=jax.ShapeDtypeStruct((depth, rows, n), F32),
        grid=(depth, n // tn),
        in_specs=[
            pl.BlockSpec((rows, d), lambda l, j: (0, 0)),
            pl.BlockSpec((None, d, tn), lambda l, j: (l, 0, j)),
            pl.BlockSpec((None, 1, tn), lambda l, j: (l, 0, j)),
        ],
        out_specs=pl.BlockSpec((None, rows, tn), lambda l, j: (l, 0, j)),
        compiler_params=_params("parallel", "parallel"),
        name="adaln_mod",
    )(cc, ada_w, ada_b.reshape(depth, 1, n))


class _Tokens:
    def __init__(self, n_prompt_tok, dec_seq):
        self.n_prompt_tok = n_prompt_tok
        self.dec_seq = dec_seq

    def row(self, start):
        return jnp.where(start < self.n_prompt_tok, 0, 1 + (start - self.n_prompt_tok) // self.dec_seq)

    def tile(self, candidates):
        for c in candidates:
            if self.n_prompt_tok % c == 0 and self.dec_seq % c == 0:
                return c
        raise ValueError("no token tile fits the sequence layout")


def _mod_spec(tok, tm, layer, k, d, grid_pos):
    def index_map(*ids):
        return (layer, tok.row(ids[grid_pos] * tm), k, 0, 0)
    return pl.BlockSpec((None, None, None, 1, d), index_map)


def _norm_mod(x_ref, nw_ref, sh_ref, sc_ref):
    return (_rms(x_ref[...], nw_ref[...]) * (1.0 + sc_ref[...]) + sh_ref[...]).astype(BF16)


def _nmm_kernel(x_ref, nw_ref, sh_ref, sc_ref, w_ref, o_ref, u_ref):
    @pl.when(pl.program_id(1) == 0)
    def _():
        u_ref[...] = _norm_mod(x_ref, nw_ref, sh_ref, sc_ref)

    o_ref[...] = jnp.dot(u_ref[...], w_ref[...], preferred_element_type=F32)


def _norm_mod_matmul(x, norm_w, mod5, layer, tok, w, widx):
    t, d = x.shape
    n = w.shape[2]
    tm = tok.tile((1024, 512, 256, 128))
    tn = _pick_tile(n, (1152, 1024, 896, 768, 512, 384, 256, 128))
    return pl.pallas_call(
        _nmm_kernel,
        out_shape=jax.ShapeDtypeStruct((t, n), F32),
        grid=(t // tm, n // tn),
        in_specs=[
            pl.BlockSpec((tm, d), lambda i, j: (i, 0)),
            pl.BlockSpec((1, d), lambda i, j: (0, 0)),
            _mod_spec(tok, tm, layer, 0, d, 0),
            _mod_spec(tok, tm, layer, 1, d, 0),
            pl.BlockSpec((None, d, tn), lambda i, j: (widx, 0, j)),
        ],
        out_specs=pl.BlockSpec((tm, tn), lambda i, j: (i, j)),
        scratch_shapes=[pltpu.VMEM((tm, d), BF16)],
        compiler_params=_params("parallel", "arbitrary"),
        name="norm_mod_matmul",
    )(x, norm_w.reshape(1, d), mod5, mod5, w)


def _mmres_kernel(y_ref, w_ref, x_ref, g_ref, o_ref):
    o_ref[...] = x_ref[...] + g_ref[...] * jnp.dot(y_ref[...], w_ref[...], preferred_element_type=F32)


def _matmul_residual(y, w, widx, x, mod5, layer, tok):
    t, kdim = y.shape
    d = x.shape[1]
    tm = tok.tile((512, 256, 128))
    tn = _pick_tile(d, (1024, 512, 256, 128))
    nj = d // tn

    def gate_map(j, i):
        return (layer, tok.row(i * tm), 2, 0, j)

    return pl.pallas_call(
        _mmres_kernel,
        out_shape=jax.ShapeDtypeStruct((t, d), F32),
        grid=(nj, t // tm),
        in_specs=[
            pl.BlockSpec((tm, kdim), lambda j, i: (i, 0)),
            pl.BlockSpec((None, kdim, tn), lambda j, i: (widx, 0, j)),
            pl.BlockSpec((tm, tn), lambda j, i: (i, j)),
            pl.BlockSpec((None, None, None, 1, tn), gate_map),
        ],
        out_specs=pl.BlockSpec((tm, tn), lambda j, i: (i, j)),
        compiler_params=_params("parallel", "parallel"),
        name="matmul_residual",
    )(y, w, x, mod5)


def _ffn_kernel(x_ref, nw_ref, sh_ref, sc_ref, g_ref, wg_ref, wu_ref, wd_ref, o_ref, u_ref, acc_ref):
    j = pl.program_id(1)

    @pl.when(j == 0)
    def _():
        u_ref[...] = _norm_mod(x_ref, nw_ref, sh_ref, sc_ref)
        acc_ref[...] = jnp.zeros_like(acc_ref)

    u = u_ref[...]
    a = jnp.dot(u, wg_ref[...], preferred_element_type=F32)
    b = jnp.dot(u, wu_ref[...], preferred_element_type=F32)
    h = (_silu(a) * b).astype(BF16)
    acc_ref[...] += jnp.dot(h, wd_ref[...], preferred_element_type=F32)

    @pl.when(j == pl.num_programs(1) - 1)
    def _():
        o_ref[...] = x_ref[...] + g_ref[...] * acc_ref[...]


def _ffn(x, norm_w, mod5, layer, tok, wg, wu, wd):
    t, d = x.shape
    hid = wg.shape[2]
    tm = tok.tile((512, 256, 128))
    th = _pick_tile(hid, (512, 256, 128))
    return pl.pallas_call(
        _ffn_kernel,
        out_shape=jax.ShapeDtypeStruct((t, d), F32),
        grid=(t // tm, hid // th),
        in_specs=[
            pl.BlockSpec((tm, d), lambda i, j: (i, 0)),
            pl.BlockSpec((1, d), lambda i, j: (0, 0)),
            _mod_spec(tok, tm, layer, 3, d, 0),
            _mod_spec(tok, tm, layer, 4, d, 0),
            _mod_spec(tok, tm, layer, 5, d, 0),
            pl.BlockSpec((None, d, th), lambda i, j: (layer, 0, j)),
            pl.BlockSpec((None, d, th), lambda i, j: (layer, 0, j)),
            pl.BlockSpec((None, th, d), lambda i, j: (layer, j, 0)),
        ],
        out_specs=pl.BlockSpec((tm, d), lambda i, j: (i, 0)),
        scratch_shapes=[pltpu.VMEM((tm, d), BF16), pltpu.VMEM((tm, d), F32)],
        compiler_params=_params("parallel", "arbitrary"),
        name="ffn",
    )(x, norm_w.reshape(1, d), mod5, mod5, mod5, wg, wu, wd)


def _ssd_dt_kernel(heads, raw_ref, bias_ref, alog_ref, dt_ref, acs_ref):
    forward = lax.broadcasted_iota(jnp.int32, (1, raw_ref.shape[1]), 1) < heads
    neg_a = -jnp.exp(alog_ref[...])
    for c in range(raw_ref.shape[0] // CHUNK):
        rows = slice(c * CHUNK, (c + 1) * CHUNK)
        dt = _softplus(raw_ref[rows, :] + bias_ref[...])
        a = dt * neg_a
        dt_ref[rows, :] = dt
        acs_ref[rows, :] = jnp.where(forward, _cumsum_chunk(a, False), _cumsum_chunk(a, True))


def _ssd_dt(dt_raw, dt_bias, a_log):
    t, cols = dt_raw.shape
    tm = _pick_tile(t, (1024, 512, 256, 128))
    return pl.pallas_call(
        functools.partial(_ssd_dt_kernel, cols // 2),
        out_shape=[jax.ShapeDtypeStruct((t, cols), F32)] * 2,
        grid=(t // tm,),
        in_specs=[
            pl.BlockSpec((tm, cols), lambda i: (i, 0)),
            pl.BlockSpec((1, cols), lambda i: (0, 0)),
            pl.BlockSpec((1, cols), lambda i: (0, 0)),
        ],
        out_specs=[pl.BlockSpec((tm, cols), lambda i: (i, 0))] * 2,
        compiler_params=_params("parallel"),
        name="ssd_dt",
    )(dt_raw, dt_bias.reshape(1, cols), a_log.reshape(1, cols))


def _ssd_kernel(nc, has_h0, has_ht, *refs):
    (z_ref, x_ref, b_ref, c_ref, acsc_ref, dtr_ref, acsr_ref, dsk_ref, nw_ref,
     cwx_ref, cwb_ref, cwc_ref, cbx_ref, cbb_ref, cbc_ref) = refs[:15]
    pos = 15
    h0_ref = None
    ht_ref = None
    if has_h0:
        h0_ref = refs[pos]
        pos += 1
    y_ref = refs[pos]
    pos += 1
    if has_ht:
        ht_ref = refs[pos]
        pos += 1
    xb_scr, xt_scr, bs_scr, cs_scr, yf_scr, yb_scr, h_scr = refs[pos:]

    seq_len = nc * CHUNK
    pairs = x_ref.shape[1] // LANES
    row = lax.broadcasted_iota(jnp.int32, (CHUNK, 1), 0)
    upper = row < SSD_HEAD_DIM
    lo = lax.broadcasted_iota(jnp.int32, (1, LANES), 1) < SSD_HEAD_DIM
    ti = lax.broadcasted_iota(jnp.int32, (CHUNK, CHUNK), 0)
    si = lax.broadcasted_iota(jnp.int32, (CHUNK, CHUNK), 1)

    def conv_silu(src, w_ref, bias_ref, c, r0):
        rp = pl.multiple_of(jnp.maximum(r0 - SUBLANES, 0), SUBLANES)
        rn = pl.multiple_of(jnp.minimum(r0 + CHUNK, seq_len - SUBLANES), SUBLANES)
        cur = src[pl.ds(r0, CHUNK), :]
        prow = jnp.where(c > 0, src[pl.ds(rp, SUBLANES), :][SUBLANES - 1:SUBLANES, :], 0.0)
        nrow = jnp.where(c < nc - 1, src[pl.ds(rn, SUBLANES), :][0:1, :], 0.0)
        xm1 = jnp.where(row == 0, prow, pltpu.roll(cur, 1, 0))
        xp1 = jnp.where(row == CHUNK - 1, nrow, pltpu.roll(cur, CHUNK - 1, 0))
        w = w_ref[...]
        return _silu(w[0:1, :] * xm1 + w[1:2, :] * cur + w[2:3, :] * xp1 + bias_ref[...])

    def conv_chunk(c, carry):
        r0 = pl.multiple_of(c * CHUNK, CHUNK)
        xs = conv_silu(x_ref, cwx_ref, cbx_ref, c, r0)
        xb_scr[pl.ds(r0, CHUNK), :] = xs.astype(BF16)
        yf_scr[pl.ds(r0, CHUNK), :] = dsk_ref[...] * xs
        for p in range(pairs):
            cols = slice(LANES * p, LANES * (p + 1))
            xt_scr[cols, pl.ds(r0, CHUNK)] = xs[:, cols].T
        bs_scr[pl.ds(r0, CHUNK), :] = conv_silu(b_ref, cwb_ref, cbb_ref, c, r0).astype(BF16)
        cs_scr[pl.ds(r0, CHUNK), :] = conv_silu(c_ref, cwc_ref, cbc_ref, c, r0).astype(BF16)
        return carry

    def chunk_step(c, d):
        r0 = pl.multiple_of(c * CHUNK, CHUNK)
        xb = xb_scr[pl.ds(r0, CHUNK), :]
        bc = bs_scr[pl.ds(r0, CHUNK), :]
        cc = cs_scr[pl.ds(r0, CHUNK), :]
        acs_c = acsc_ref[pl.ds(r0, CHUNK), :]
        acs_r = acsr_ref[:, pl.ds(r0, CHUNK)]
        dt_r = dtr_ref[:, pl.ds(r0, CHUNK)]
        last = CHUNK - 1 if d == 0 else 0
        end_r = jnp.broadcast_to(acs_r[:, last:last + 1], acs_r.shape)
        w_r = dt_r * jnp.exp(end_r - acs_r)
        tot_r = jnp.exp(end_r)
        cb = lax.dot_general(cc, bc, NT_DIMS, preferred_element_type=F32)
        mask = (ti >= si) if d == 0 else (ti <= si)

        def head(k):
            col = jnp.broadcast_to(acs_c[:, k:k + 1], (CHUNK, CHUNK))
            decay = jnp.exp(jnp.where(mask, col - acs_r[k:k + 1, :], -jnp.inf)) * dt_r[k:k + 1, :]
            return col, (cb * decay).astype(BF16)

        for p in range(pairs):
            k0 = SSD_HEADS_PER_GROUP * d + 2 * p
            k1 = k0 + 1
            cols = slice(LANES * p, LANES * (p + 1))
            col0, m0 = head(k0)
            col1, m1 = head(k1)
            xbp = xb[:, cols]
            zero = jnp.zeros_like(xbp)
            y = jnp.dot(m0, jnp.where(lo, xbp, zero), preferred_element_type=F32)
            y += jnp.dot(m1, jnp.where(lo, zero, xbp), preferred_element_type=F32)
            hp = h_scr[d, cols, :]
            y_inter = lax.dot_general(cc, hp.astype(BF16), NT_DIMS, preferred_element_type=F32)
            y += y_inter * jnp.exp(jnp.where(lo, col0, col1))
            xw = xt_scr[cols, pl.ds(r0, CHUNK)] * jnp.where(upper, w_r[k0:k0 + 1, :], w_r[k1:k1 + 1, :])
            s_new = jnp.dot(xw.astype(BF16), bc, preferred_element_type=F32)
            h_scr[d, cols, :] = jnp.where(upper, tot_r[k0:k0 + 1, :], tot_r[k1:k1 + 1, :]) * hp + s_new
            if d == 0:
                yf_scr[pl.ds(r0, CHUNK), cols] += y
            else:
                yb_scr[pl.ds(r0, CHUNK), cols] = y

    if has_h0:
        h_scr[...] = h0_ref[...]
    else:
        h_scr[...] = jnp.zeros_like(h_scr)
    lax.fori_loop(0, nc, conv_chunk, 0)

    def scan_body(i, carry):
        chunk_step(i, 0)
        chunk_step(nc - 1 - i, 1)
        return carry

    lax.fori_loop(0, nc, scan_body, 0)

    def finish(c, carry):
        r0 = pl.multiple_of(c * CHUNK, CHUNK)
        y = yf_scr[pl.ds(r0, CHUNK), :] + yb_scr[pl.ds(r0, CHUNK), :]
        g = y * _silu(z_ref[pl.ds(r0, CHUNK), :])
        y_ref[pl.ds(r0, CHUNK), :] = _rms(g, nw_ref[...]).astype(y_ref.dtype)
        return carry

    lax.fori_loop(0, nc, finish, 0)
    if has_ht:
        ht_ref[...] = h_scr[...]


def _ssd_scan(proj, acs_c, dt_r, acs_r, dsk, norm_w, conv_w, conv_b, row0, nseq, seq_len, h0, want_state):
    groups = acs_c.shape[0]
    gw = SSD_HEADS_PER_GROUP * SSD_HEAD_DIM
    d_inner = groups * gw
    n = SSD_D_STATE
    nc = seq_len // CHUNK
    rb0 = row0 // seq_len
    xoff = d_inner // gw
    boff = 2 * d_inner // n
    coff = boff + groups
    cwx_off = 0
    cwb_off = d_inner // n
    cwc_off = cwb_off + groups

    in_specs = [
        pl.BlockSpec((seq_len, gw), lambda b, g: (rb0 + b, g)),
        pl.BlockSpec((seq_len, gw), lambda b, g: (rb0 + b, xoff + g)),
        pl.BlockSpec((seq_len, n), lambda b, g: (rb0 + b, boff + g)),
        pl.BlockSpec((seq_len, n), lambda b, g: (rb0 + b, coff + g)),
        pl.BlockSpec((None, seq_len, 2 * SSD_HEADS_PER_GROUP), lambda b, g: (g, rb0 + b, 0)),
        pl.BlockSpec((None, 2 * SSD_HEADS_PER_GROUP, seq_len), lambda b, g: (g, 0, rb0 + b)),
        pl.BlockSpec((None, 2 * SSD_HEADS_PER_GROUP, seq_len), lambda b, g: (g, 0, rb0 + b)),
        pl.BlockSpec((1, gw), lambda b, g: (0, g)),
        pl.BlockSpec((1, gw), lambda b, g: (0, g)),
        pl.BlockSpec((SSD_CONV_W, gw), lambda b, g: (0, cwx_off + g)),
        pl.BlockSpec((SSD_CONV_W, n), lambda b, g: (0, cwb_off + g)),
        pl.BlockSpec((SSD_CONV_W, n), lambda b, g: (0, cwc_off + g)),
        pl.BlockSpec((1, gw), lambda b, g: (0, cwx_off + g)),
        pl.BlockSpec((1, n), lambda b, g: (0, cwb_off + g)),
        pl.BlockSpec((1, n), lambda b, g: (0, cwc_off + g)),
    ]
    args = [proj, proj, proj, proj, acs_c, dt_r, acs_r, dsk, norm_w,
            conv_w, conv_w, conv_w, conv_b, conv_b, conv_b]
    if h0 is not None:
        in_specs.append(pl.BlockSpec((None, 2, gw, n), lambda b, g: (b, 0, g, 0)))
        args.append(h0)
    out_shape = [jax.ShapeDtypeStruct((nseq * seq_len, d_inner), BF16)]
    out_specs = [pl.BlockSpec((seq_len, gw), lambda b, g: (b, g))]
    if want_state:
        out_shape.append(jax.ShapeDtypeStruct((nseq, 2, d_inner, n), F32))
        out_specs.append(pl.BlockSpec((None, 2, gw, n), lambda b, g: (b, 0, g, 0)))
    outs = pl.pallas_call(
        functools.partial(_ssd_kernel, nc, h0 is not None, want_state),
        out_shape=out_shape,
        grid=(nseq, groups),
        in_specs=in_specs,
        out_specs=out_specs,
        scratch_shapes=[
            pltpu.VMEM((seq_len, gw), BF16),
            pltpu.VMEM((gw, seq_len), F32),
            pltpu.VMEM((seq_len, n), BF16),
            pltpu.VMEM((seq_len, n), BF16),
            pltpu.VMEM((seq_len, gw), F32),
            pltpu.VMEM((seq_len, gw), F32),
            pltpu.VMEM((2, gw, n), F32),
        ],
        compiler_params=_params("parallel", "parallel"),
        name="ssd_scan",
    )(*args)
    return outs if want_state else (outs[0], None)


def _ssd_mixer(x, norm_w_mix, mod5, layer, tok, geo, h0_sample, j, w_in, conv_w, conv_b, dt_bias, a_log,
               d_skip, norm_w, w_out):
    heads = dt_bias.shape[1]
    groups = heads // SSD_HEADS_PER_GROUP
    d_inner = heads * SSD_HEAD_DIM
    hpg = SSD_HEADS_PER_GROUP
    t = x.shape[0]
    proj = _norm_mod_matmul(x, norm_w_mix, mod5, layer, tok, w_in, j)
    dt_off = d_inner + conv_w.shape[1]
    dt, acs = _ssd_dt(proj[:, dt_off:dt_off + 2 * heads], dt_bias, a_log)

    def per_group(v):
        return v.reshape(t, 2, groups, hpg).transpose(2, 0, 1, 3).reshape(groups, t, 2 * hpg)

    acs_c = per_group(acs)
    acs_r = acs_c.transpose(0, 2, 1)
    dt_r = per_group(dt).transpose(0, 2, 1)
    dsk = jnp.repeat(d_skip, SSD_HEAD_DIM).reshape(1, d_inner)
    nw = norm_w.reshape(1, d_inner)
    cb = conv_b.reshape(1, -1)
    bp, lp, bs, ls = geo
    y_p, st = _ssd_scan(proj, acs_c, dt_r, acs_r, dsk, nw, conv_w, cb, 0, bp, lp, None, True)
    h0 = h0_sample.reshape(bs, 2, d_inner, SSD_D_STATE)
    y_s, _ = _ssd_scan(proj, acs_c, dt_r, acs_r, dsk, nw, conv_w, cb, bp * lp, bs, ls, h0, False)
    y = jnp.concatenate([y_p, y_s], axis=0)
    x = _matmul_residual(y, w_out, j, x, mod5, layer, tok)
    return x, st.reshape(bp, 2, heads, SSD_HEAD_DIM, SSD_D_STATE)


def _rope(x, cos, sin):
    quarter = ATTN_HEAD_DIM // 4
    lane = lax.broadcasted_iota(jnp.int32, (1, ATTN_HEAD_DIM), 1)
    first = (lane % (2 * quarter)) < quarter
    swapped = jnp.where(first, pltpu.roll(x, ATTN_HEAD_DIM - quarter, 1), pltpu.roll(x, quarter, 1))
    return x * cos + swapped * sin


def _attn_kernel(nq, has_cache, use_rope, write_k, *refs):
    q_ref, k_ref, v_ref, qn_ref, kn_ref = refs[:5]
    pos = 5
    cos_ref = sin_ref = kc_ref = vc_ref = knew_ref = None
    if use_rope:
        cos_ref, sin_ref = refs[pos:pos + 2]
        pos += 2
    if has_cache:
        kc_ref, vc_ref = refs[pos:pos + 2]
        pos += 2
    o_ref = refs[pos]
    pos += 1
    if write_k:
        knew_ref = refs[pos]
        pos += 1
    kb_scr, vb_scr = refs[pos:pos + 2]
    pos += 2
    if has_cache:
        kcb_scr, vcb_scr = refs[pos:pos + 2]
        kcb_scr[...] = kc_ref[...].astype(BF16)
        vcb_scr[...] = vc_ref[...].astype(BF16)

    grp = q_ref.shape[1] // ATTN_HEAD_DIM
    scale = ATTN_HEAD_DIM ** -0.5

    def kprep(c, carry):
        r0 = pl.multiple_of(c * CHUNK, CHUNK)
        kn = _rms(k_ref[pl.ds(r0, CHUNK), :], kn_ref[...])
        if write_k:
            knew_ref[pl.ds(r0, CHUNK), :] = kn
        if use_rope:
            kn = _rope(kn, cos_ref[pl.ds(r0, CHUNK), :], sin_ref[pl.ds(r0, CHUNK), :])
        kb_scr[pl.ds(r0, CHUNK), :] = kn.astype(BF16)
        vb_scr[pl.ds(r0, CHUNK), :] = v_ref[pl.ds(r0, CHUNK), :].astype(BF16)
        return carry

    lax.fori_loop(0, nq, kprep, 0)

    def qblock(c, carry):
        r0 = pl.multiple_of(c * CHUNK, CHUNK)
        for g in range(grp):
            cols = slice(ATTN_HEAD_DIM * g, ATTN_HEAD_DIM * (g + 1))
            qn = _rms(q_ref[pl.ds(r0, CHUNK), cols], qn_ref[...])
            if use_rope:
                qn = _rope(qn, cos_ref[pl.ds(r0, CHUNK), :], sin_ref[pl.ds(r0, CHUNK), :])
            qb = (qn * (scale * LOG2_E)).astype(BF16)
            s = lax.dot_general(qb, kb_scr[...], NT_DIMS, preferred_element_type=F32)
            m = jnp.max(s, axis=1, keepdims=True)
            if has_cache:
                sc = lax.dot_general(qb, kcb_scr[...], NT_DIMS, preferred_element_type=F32)
                m = jnp.maximum(m, jnp.max(sc, axis=1, keepdims=True))
                ec = jnp.exp2(sc - m)
            e = jnp.exp2(s - m)
            den = jnp.sum(e, axis=1, keepdims=True)
            if has_cache:
                den = den + jnp.sum(ec, axis=1, keepdims=True)
            rcp = 1.0 / den
            o = jnp.dot((e * rcp).astype(BF16), vb_scr[...], preferred_element_type=F32)
            if has_cache:
                o += jnp.dot((ec * rcp).astype(BF16), vcb_scr[...], preferred_element_type=F32)
            o_ref[pl.ds(r0, CHUNK), cols] = o.astype(o_ref.dtype)
        return carry

    lax.fori_loop(0, nq, qblock, 0)


def _attention(qkv, q_norm, k_norm, heads, kv_heads, row0, nseq, seq_len, rope, cache, write_k):
    dh = ATTN_HEAD_DIM
    grp = heads // kv_heads
    rb0 = row0 // seq_len
    nq = seq_len // CHUNK
    in_specs = [
        pl.BlockSpec((seq_len, grp * dh), lambda b, h: (rb0 + b, h)),
        pl.BlockSpec((seq_len, dh), lambda b, h: (rb0 + b, heads + h)),
        pl.BlockSpec((seq_len, dh), lambda b, h: (rb0 + b, heads + kv_heads + h)),
        pl.BlockSpec((1, dh), lambda b, h: (0, 0)),
        pl.BlockSpec((1, dh), lambda b, h: (0, 0)),
    ]
    args = [qkv, qkv, qkv, q_norm.reshape(1, dh), k_norm.reshape(1, dh)]
    scratch = [pltpu.VMEM((seq_len, dh), BF16), pltpu.VMEM((seq_len, dh), BF16)]
    if rope is not None:
        in_specs += [pl.BlockSpec((seq_len, dh), lambda b, h: (0, 0))] * 2
        args += list(rope)
    if cache is not None:
        past = cache[0].shape[1]
        in_specs += [pl.BlockSpec((None, past, dh), lambda b, h: (b, 0, h))] * 2
        args += list(cache)
        scratch += [pltpu.VMEM((past, dh), BF16), pltpu.VMEM((past, dh), BF16)]
    out_shape = [jax.ShapeDtypeStruct((nseq * seq_len, heads * dh), BF16)]
    out_specs = [pl.BlockSpec((seq_len, grp * dh), lambda b, h: (b, h))]
    if write_k:
        out_shape.append(jax.ShapeDtypeStruct((nseq * seq_len, kv_heads * dh), F32))
        out_specs.append(pl.BlockSpec((seq_len, dh), lambda b, h: (b, h)))
    outs = pl.pallas_call(
        functools.partial(_attn_kernel, nq, cache is not None, rope is not None, write_k),
        out_shape=out_shape,
        grid=(nseq, kv_heads),
        in_specs=in_specs,
        out_specs=out_specs,
        scratch_shapes=scratch,
        compiler_params=_params("parallel", "parallel"),
        name="attention",
    )(*args)
    return outs if write_k else (outs[0], None)


def _rope_tables(seq_len):
    quarter = ATTN_HEAD_DIM // 4
    t = jnp.arange(seq_len)
    lane = jnp.arange(ATTN_HEAD_DIM)
    pos = jnp.where(lane[None, :] < 2 * quarter, (t // GRID_W)[:, None], (t % GRID_W)[:, None]).astype(F32)
    inv_freq = ROPE_THETA ** (-jnp.arange(quarter, dtype=F32) / quarter)
    ang = pos * inv_freq[lane % quarter][None, :]
    sign = jnp.where((lane % (2 * quarter)) < quarter, -1.0, 1.0).astype(F32)
    return jnp.cos(ang), jnp.sin(ang) * sign[None, :]


def _attn_mixer(x, norm_w_mix, mod5, layer, tok, geo, cache_k, cache_v, j, w_qkv, q_norm, k_norm, w_out):
    bp, lp, bs, ls = geo
    dh = ATTN_HEAD_DIM
    kv_heads = cache_k.shape[2]
    heads = w_out.shape[1] // dh
    qkv = _norm_mod_matmul(x, norm_w_mix, mod5, layer, tok, w_qkv, j)
    o_p, k_new = _attention(qkv, q_norm, k_norm, heads, kv_heads, 0, bp, lp, None, None, True)
    past = cache_k.shape[1]
    cache = (cache_k.reshape(bs, past, kv_heads * dh), cache_v.reshape(bs, past, kv_heads * dh))
    o_s, _ = _attention(qkv, q_norm, k_norm, heads, kv_heads, bp * lp, bs, ls, _rope_tables(ls), cache, False)
    o = jnp.concatenate([o_p, o_s], axis=0)
    x = _matmul_residual(o, w_out, j, x, mod5, layer, tok)
    v_new = qkv[:bp * lp, (heads + kv_heads) * dh:]
    return x, k_new.reshape(bp, lp, kv_heads, dh), v_new.reshape(bp, lp, kv_heads, dh)


def _mlstm_gate_kernel(heads, i_ref, f_ref, bi_ref, bf_ref, b_ref, r_ref, p_ref):
    forward = lax.broadcasted_iota(jnp.int32, (1, i_ref.shape[1]), 1) < heads
    for c in range(i_ref.shape[0] // CHUNK):
        rows = slice(c * CHUNK, (c + 1) * CHUNK)
        log_f = _log_sigmoid(f_ref[rows, :] + bf_ref[...])
        b = jnp.where(forward, _cumsum_chunk(log_f, False), _cumsum_chunk(log_f, True))
        r = i_ref[rows, :] + bi_ref[...] - b
        b_ref[rows, :] = b
        r_ref[rows, :] = r
        p_ref[rows, :] = jnp.where(forward, _cummax_chunk(r, False), _cummax_chunk(r, True))


def _mlstm_gates(i_raw, f_raw, b_i, b_f):
    t, cols = i_raw.shape
    tm = _pick_tile(t, (1024, 512, 256, 128))
    return pl.pallas_call(
        functools.partial(_mlstm_gate_kernel, cols // 2),
        out_shape=[jax.ShapeDtypeStruct((t, cols), F32)] * 3,
        grid=(t // tm,),
        in_specs=[
            pl.BlockSpec((tm, cols), lambda i: (i, 0)),
            pl.BlockSpec((tm, cols), lambda i: (i, 0)),
            pl.BlockSpec((1, cols), lambda i: (0, 0)),
            pl.BlockSpec((1, cols), lambda i: (0, 0)),
        ],
        out_specs=[pl.BlockSpec((tm, cols), lambda i: (i, 0))] * 3,
        compiler_params=_params("parallel"),
        name="mlstm_gates",
    )(i_raw, f_raw, b_i, b_f)


def _mlstm_kernel(nc, has_s0, has_st, *refs):
    q_ref, k_ref, v_ref, og_ref, cp_ref, rr_ref, hn_ref = refs[:7]
    pos = 7
    c0_ref = n0_ref = m0_ref = ct_ref = nt_ref = mt_ref = None
    if has_s0:
        c0_ref, n0_ref, m0_ref = refs[pos:pos + 3]
        pos += 3
    h_ref = refs[pos]
    pos += 1
    if has_st:
        ct_ref, nt_ref, mt_ref = refs[pos:pos + 3]
        pos += 3
    hf_scr, hb_scr, c_scr, n_scr, m_scr = refs[pos:]

    dk = q_ref.shape[1]
    dv = v_ref.shape[1]
    qscale = dk ** -0.5
    tile = (CHUNK, CHUNK)
    ti = lax.broadcasted_iota(jnp.int32, tile, 0)
    si = lax.broadcasted_iota(jnp.int32, tile, 1)

    def wide(x):
        return jnp.concatenate([x] * (dv // CHUNK), axis=1)

    def chunk_step(c, d):
        r0 = pl.multiple_of(c * CHUNK, CHUNK)
        q = q_ref[pl.ds(r0, CHUNK), :] * qscale
        k = k_ref[pl.ds(r0, CHUNK), :]
        qb = q.astype(BF16)
        kb = k.astype(BF16)
        vb = v_ref[pl.ds(r0, CHUNK), :].astype(BF16)
        cp = cp_ref[pl.ds(r0, CHUNK), :]
        b_t = jnp.broadcast_to(cp[:, 3 * d:3 * d + 1], tile)
        r_t = jnp.broadcast_to(cp[:, 3 * d + 1:3 * d + 2], tile)
        p_t = jnp.broadcast_to(cp[:, 3 * d + 2:3 * d + 3], tile)
        r_row = rr_ref[d:d + 1, pl.ds(r0, CHUNK)]
        m_row = m_scr[d]
        u_t = jnp.maximum(m_row, p_t)
        mask = (ti >= si) if d == 0 else (ti <= si)
        s = lax.dot_general(qb, kb, NT_DIMS, preferred_element_type=F32)
        s = s * jnp.exp(jnp.where(mask, r_row - u_t, -jnp.inf))
        inter = jnp.exp(m_row - u_t)
        cst = c_scr[d]
        nst = n_scr[d]
        num = jnp.dot(s.astype(BF16), vb, preferred_element_type=F32)
        num += wide(inter) * jnp.dot(qb, cst.astype(BF16), preferred_element_type=F32)
        den = jnp.sum(s + inter * (q * nst), axis=1, keepdims=True)
        rcp = 1.0 / jnp.maximum(jnp.abs(den), jnp.exp(-(b_t + u_t)))
        h = num * wide(rcp)
        last = CHUNK - 1 if d == 0 else 0
        u_last = u_t[last:last + 1, :]
        wk = jnp.exp(r_t - u_last) * k
        carry_decay = jnp.exp(m_row - u_last)
        c_scr[d] = wide(carry_decay) * cst + jnp.dot(wk.T.astype(BF16), vb, preferred_element_type=F32)
        n_scr[d] = carry_decay * nst + jnp.sum(wk, axis=0, keepdims=True)
        m_scr[d] = b_t[last:last + 1, :] + u_last
        (hf_scr if d == 0 else hb_scr)[pl.ds(r0, CHUNK), :] = h

    for d in range(2):
        if has_s0:
            c_scr[d] = c0_ref[d]
            n_scr[d] = n0_ref[d]
            m_scr[d] = m0_ref[d]
        else:
            c_scr[d] = jnp.zeros(c_scr.shape[1:], F32)
            n_scr[d] = jnp.zeros(n_scr.shape[1:], F32)
            m_scr[d] = jnp.zeros(m_scr.shape[1:], F32)

    def scan_body(i, carry):
        chunk_step(i, 0)
        chunk_step(nc - 1 - i, 1)
        return carry

    lax.fori_loop(0, nc, scan_body, 0)

    def finish(c, carry):
        r0 = pl.multiple_of(c * CHUNK, CHUNK)
        hs = _rms(hf_scr[pl.ds(r0, CHUNK), :] + hb_scr[pl.ds(r0, CHUNK), :], hn_ref[...])
        h_ref[pl.ds(r0, CHUNK), :] = (hs * _sigmoid(og_ref[pl.ds(r0, CHUNK), :])).astype(h_ref.dtype)
        return carry

    lax.fori_loop(0, nc, finish, 0)
    if has_st:
        for d in range(2):
            ct_ref[d] = c_scr[d]
            nt_ref[d] = n_scr[d]
            mt_ref[d] = m_scr[d]


def _mlstm_scan(proj, gate_cols, gate_rows, head_norm, heads, dk, dv, row0, nseq, seq_len, state, want_state):
    assert dv == 2 * dk and dk == CHUNK
    nc = seq_len // CHUNK
    rb0 = row0 // seq_len
    in_specs = [
        pl.BlockSpec((seq_len, dk), lambda b, h: (rb0 + b, h)),
        pl.BlockSpec((seq_len, dk), lambda b, h: (rb0 + b, heads + h)),
        pl.BlockSpec((seq_len, dv), lambda b, h: (rb0 + b, heads + h)),
        pl.BlockSpec((seq_len, dv), lambda b, h: (rb0 + b, 2 * heads + h)),
        pl.BlockSpec((None, seq_len, 6), lambda b, h: (h, rb0 + b, 0)),
        pl.BlockSpec((None, 2, seq_len), lambda b, h: (h, 0, rb0 + b)),
        pl.BlockSpec((None, 1, dv), lambda b, h: (h, 0, 0)),
    ]
    args = [proj, proj, proj, proj, gate_cols, gate_rows, head_norm]
    state_specs = [
        pl.BlockSpec((None, 2, None, dk, dv), lambda b, h: (b, 0, h, 0, 0)),
        pl.BlockSpec((None, 2, None, 1, dk), lambda b, h: (b, 0, h, 0, 0)),
        pl.BlockSpec((None, 2, None, 1, LANES), lambda b, h: (b, 0, h, 0, 0)),
    ]
    if state is not None:
        in_specs += state_specs
        args += list(state)
    out_shape = [jax.ShapeDtypeStruct((nseq * seq_len, heads * dv), BF16)]
    out_specs = [pl.BlockSpec((seq_len, dv), lambda b, h: (b, h))]
    if want_state:
        out_shape += [jax.ShapeDtypeStruct((nseq, 2, heads, dk, dv), F32),
                      jax.ShapeDtypeStruct((nseq, 2, heads, 1, dk), F32),
                      jax.ShapeDtypeStruct((nseq, 2, heads, 1, LANES), F32)]
        out_specs += state_specs
    outs = pl.pallas_call(
        functools.partial(_mlstm_kernel, nc, state is not None, want_state),
        out_shape=out_shape,
        grid=(nseq, heads),
        in_specs=in_specs,
        out_specs=out_specs,
        scratch_shapes=[
            pltpu.VMEM((seq_len, dv), F32),
            pltpu.VMEM((seq_len, dv), F32),
            pltpu.VMEM((2, dk, dv), F32),
            pltpu.VMEM((2, 1, dk), F32),
            pltpu.VMEM((2, 1, LANES), F32),
        ],
        compiler_params=_params("parallel", "parallel"),
        name="mlstm_scan",
    )(*args)
    return outs


def _mlstm_mixer(x, norm_w_mix, mod5, layer, tok, geo, c0, n0, m0, j, w_in, b_gates, head_norm, w_out):
    bp, lp, bs, ls = geo
    heads, dk, dv = c0.shape[2], c0.shape[3], c0.shape[4]
    t = x.shape[0]
    proj = _norm_mod_matmul(x, norm_w_mix, mod5, layer, tok, w_in, j)
    g_off = 2 * heads * dk + 2 * heads * dv
    graw = proj[:, g_off:g_off + 4 * heads].reshape(t, 2, 2, heads)
    b, r, p = _mlstm_gates(graw[:, :, 0, :].reshape(t, 2 * heads), graw[:, :, 1, :].reshape(t, 2 * heads),
                           b_gates[:, 0, :].reshape(1, 2 * heads), b_gates[:, 1, :].reshape(1, 2 * heads))
    gate_cols = jnp.stack([b, r, p], axis=-1).reshape(t, 2, heads, 3).transpose(2, 0, 1, 3).reshape(heads, t, 6)
    gate_rows = r.reshape(t, 2, heads).transpose(2, 1, 0)
    hn = head_norm.reshape(heads, 1, dv)
    outs_p = _mlstm_scan(proj, gate_cols, gate_rows, hn, heads, dk, dv, 0, bp, lp, None, True)
    h_p, c_t, n_t, m_t = outs_p
    state = (c0, n0.reshape(bs, 2, heads, 1, dk),
             jnp.broadcast_to(m0[..., None, None], (bs, 2, heads, 1, LANES)))
    (h_s,) = _mlstm_scan(proj, gate_cols, gate_rows, hn, heads, dk, dv, bp * lp, bs, ls, state, False)
    h = jnp.concatenate([h_p, h_s], axis=0)
    x = _matmul_residual(h, w_out, j, x, mod5, layer, tok)
    return x, c_t, n_t.reshape(bp, 2, heads, dk), m_t[:, :, :, 0, 0]


def kernel(x_prompt, x_sample, c, state_ssd, cache_attn_k, cache_attn_v, state_mlstm_C, state_mlstm_n, state_mlstm_m, c_ctx, ada_w, ada_b, norm_mix_w, norm_ffn_w, ffn_w_gate, ffn_w_up, ffn_w_down, ssd_w_in, ssd_conv_w, ssd_conv_b, ssd_dt_bias, ssd_a_log, ssd_d, ssd_norm_w, ssd_w_out, attn_w_qkv, attn_q_norm, attn_k_norm, attn_w_out, mlstm_w_in, mlstm_b_gates, mlstm_head_norm, mlstm_w_out):
    bp, lp, d = x_prompt.shape
    bs, ls, _ = x_sample.shape
    depth = ada_w.shape[0]
    geo = (bp, lp, bs, ls)
    tok = _Tokens(bp * lp, ls)
    x = jnp.concatenate([x_prompt.reshape(bp * lp, d), x_sample.reshape(bs * ls, d)], axis=0)

    mod_rows = -(-(1 + bs) // SUBLANES) * SUBLANES
    cc = jnp.concatenate([c_ctx[None, :], c, jnp.zeros((mod_rows - 1 - bs, d), F32)], axis=0)
    mod5 = _modulation(cc, ada_w, ada_b).reshape(depth, mod_rows, 6, 1, d)

    w_gate, w_up, w_down = _to_bf16(ffn_w_gate), _to_bf16(ffn_w_up), _to_bf16(ffn_w_down)
    ssd_in, ssd_out = _to_bf16(ssd_w_in), _to_bf16(ssd_w_out)
    attn_qkv, attn_out = _to_bf16(attn_w_qkv), _to_bf16(attn_w_out)
    mlstm_cols = mlstm_w_in.shape[2]
    mlstm_in = jnp.pad(_to_bf16(mlstm_w_in), ((0, 0), (0, 0), (0, -mlstm_cols % LANES)))
    mlstm_out = _to_bf16(mlstm_w_out)
    new_ssd, new_k, new_v, new_c, new_n, new_m = [], [], [], [], [], []
    for l in range(depth):
        kind, j = l % 3, l // 3
        if kind == 0:
            x, st = _ssd_mixer(x, norm_mix_w[l], mod5, l, tok, geo, state_ssd[:, j], j, ssd_in, ssd_conv_w[j],
                               ssd_conv_b[j], ssd_dt_bias[j], ssd_a_log[j], ssd_d[j], ssd_norm_w[j], ssd_out)
            new_ssd.append(st)
        elif kind == 1:
            x, k_new, v_new = _attn_mixer(x, norm_mix_w[l], mod5, l, tok, geo, cache_attn_k[:, j], cache_attn_v[:, j],
                                          j, attn_qkv, attn_q_norm[j], attn_k_norm[j], attn_out)
            new_k.append(k_new)
            new_v.append(v_new)
        else:
            x, c_t, n_t, m_t = _mlstm_mixer(x, norm_mix_w[l], mod5, l, tok, geo, state_mlstm_C[:, j],
                                            state_mlstm_n[:, j], state_mlstm_m[:, j], j, mlstm_in,
                                            mlstm_b_gates[j], mlstm_head_norm[j], mlstm_out)
            new_c.append(c_t)
            new_n.append(n_t)
            new_m.append(m_t)
        x = _ffn(x, norm_ffn_w[l], mod5, l, tok, w_gate, w_up, w_down)

    y_prompt = x[:bp * lp].reshape(bp, lp, d)
    y_sample = x[bp * lp:].reshape(bs, ls, d)
    return (y_prompt, y_sample, jnp.stack(new_ssd, axis=1), jnp.stack(new_k, axis=1), jnp.stack(new_v, axis=1),
            jnp.stack(new_c, axis=1), jnp.stack(new_n, axis=1), jnp.stack(new_m, axis=1))
```

```python
import functools

import jax
import jax.numpy as jnp
from jax import lax
from jax.experimental import pallas as pl
from jax.experimental.pallas import tpu as pltpu

F32 = jnp.float32
BF16 = jnp.bfloat16

EPS = 1e-6
CHUNK = 128
GRID_W = 64
ROPE_THETA = 10000.0
SSD_HEAD_DIM = 64
SSD_D_STATE = 128
SSD_CONV_W = 3
SSD_HEADS_PER_GROUP = 8
ATTN_HEAD_DIM = 128
LANES = 128
SUBLANES = 8
VMEM_LIMIT_BYTES = 56 * 1024 * 1024

NT_DIMS = (((1,), (1,)), ((), ()))
LOG2_E = 1.4426950408889634


def _sigmoid(x):
    return 0.5 + 0.5 * jnp.tanh(0.5 * x)


def _silu(x):
    return x * _sigmoid(x)


def _softplus(x):
    return jnp.maximum(x, 0.0) + jnp.log1p(jnp.exp(-jnp.abs(x)))


def _log_sigmoid(x):
    return -_softplus(-x)


def _rms(x, w):
    return x * lax.rsqrt(jnp.mean(x * x, axis=-1, keepdims=True) + EPS) * w


def _params(*sem):
    return pltpu.CompilerParams(dimension_semantics=sem, vmem_limit_bytes=VMEM_LIMIT_BYTES)


def _pick_tile(n, candidates):
    for c in candidates:
        if n % c == 0:
            return c
    return n


def _scan_chunk(a, reverse, combine, identity):
    idx = lax.broadcasted_iota(jnp.int32, (CHUNK, 1), 0)
    sh = 1
    while sh < CHUNK:
        if reverse:
            a = combine(a, jnp.where(idx < CHUNK - sh, pltpu.roll(a, CHUNK - sh, 0), identity))
        else:
            a = combine(a, jnp.where(idx >= sh, pltpu.roll(a, sh, 0), identity))
        sh *= 2
    return a


def _cumsum_chunk(a, reverse):
    return _scan_chunk(a, reverse, jnp.add, 0.0)


def _cummax_chunk(a, reverse):
    return _scan_chunk(a, reverse, jnp.maximum, -jnp.inf)


def _cast_kernel(w_ref, o_ref):
    o_ref[...] = w_ref[...].astype(BF16)


def _to_bf16(w):
    shape = w.shape
    w2 = w.reshape(-1, shape[-1])
    rows, n = w2.shape
    tr = _pick_tile(rows, (256, 128, 64, 32, 16))
    out = pl.pallas_call(
        _cast_kernel,
        out_shape=jax.ShapeDtypeStruct((rows, n), BF16),
        grid=(rows // tr,),
        in_specs=[pl.BlockSpec((tr, n), lambda i: (i, 0))],
        out_specs=pl.BlockSpec((tr, n), lambda i: (i, 0)),
        compiler_params=_params("parallel"),
        name="cast_bf16",
    )(w2)
    return out.reshape(shape)


def _mod_kernel(c_ref, w_ref, b_ref, o_ref):
    s = _silu(c_ref[...]).astype(BF16)
    o_ref[...] = jnp.dot(s, w_ref[...].astype(BF16), preferred_element_type=F32) + b_ref[...]


def _modulation(cc, ada_w, ada_b):
    depth, d, n = ada_w.shape
    rows = cc.shape[0]
    tn = _pick_tile(n, (1024, 512, 256, 128))
    return pl.pallas_call(
        _mod_kernel,
        out_shape=jax.ShapeDtypeStruct((depth, rows, n), F32),
        grid=(depth, n // tn),
        in_specs=[
            pl.BlockSpec((rows, d), lambda l, j: (0, 0)),
            pl.BlockSpec((None, d, tn), lambda l, j: (l, 0, j)),
            pl.BlockSpec((None, 1, tn), lambda l, j: (l, 0, j)),
        ],
        out_specs=pl.BlockSpec((None, rows, tn), lambda l, j: (l, 0, j)),
        compiler_params=_params("parallel", "parallel"),
        name="adaln_mod",
    )(cc, ada_w, ada_b.reshape(depth, 1, n))


class _Tokens:
    def __init__(self, n_prompt_tok, dec_seq):
        self.n_prompt_tok = n_prompt_tok
        self.dec_seq = dec_seq

    def row(self, start):
        return jnp.where(start < self.n_prompt_tok, 0, 1 + (start - self.n_prompt_tok) // self.dec_seq)

    def tile(self, candidates):
        for c in candidates:
            if self.n_prompt_tok % c == 0 and self.dec_seq % c == 0:
                return c
        raise ValueError("no token tile fits the sequence layout")


def _mod_spec(tok, tm, layer, k, d, grid_pos):
    def index_map(*ids):
        return (layer, tok.row(ids[grid_pos] * tm), k, 0, 0)
    return pl.BlockSpec((None, None, None, 1, d), index_map)


def _norm_mod(x_ref, nw_ref, sh_ref, sc_ref):
    return (_rms(x_ref[...], nw_ref[...]) * (1.0 + sc_ref[...]) + sh_ref[...]).astype(BF16)


def _nmm_kernel(x_ref, nw_ref, sh_ref, sc_ref, w_ref, o_ref, u_ref):
    @pl.when(pl.program_id(1) == 0)
    def _():
        u_ref[...] = _norm_mod(x_ref, nw_ref, sh_ref, sc_ref)

    o_ref[...] = jnp.dot(u_ref[...], w_ref[...], preferred_element_type=F32)


def _norm_mod_matmul(x, norm_w, mod5, layer, tok, w, widx):
    t, d = x.shape
    n = w.shape[2]
    tm = tok.tile((1024, 512, 256, 128))
    tn = _pick_tile(n, (1152, 1024, 896, 768, 512, 384, 256, 128))
    return pl.pallas_call(
        _nmm_kernel,
        out_shape=jax.ShapeDtypeStruct((t, n), F32),
        grid=(t // tm, n // tn),
        in_specs=[
            pl.BlockSpec((tm, d), lambda i, j: (i, 0)),
            pl.BlockSpec((1, d), lambda i, j: (0, 0)),
            _mod_spec(tok, tm, layer, 0, d, 0),
            _mod_spec(tok, tm, layer, 1, d, 0),
            pl.BlockSpec((None, d, tn), lambda i, j: (widx, 0, j)),
        ],
        out_specs=pl.BlockSpec((tm, tn), lambda i, j: (i, j)),
        scratch_shapes=[pltpu.VMEM((tm, d), BF16)],
        compiler_params=_params("parallel", "arbitrary"),
        name="norm_mod_matmul",
    )(x, norm_w.reshape(1, d), mod5, mod5, w)


def _mmres_kernel(prompt_tiles, yp_ref, ys_ref, w_ref, x_ref, g_ref, o_ref):
    def run(y_ref):
        o_ref[...] = x_ref[...] + g_ref[...] * jnp.dot(y_ref[...], w_ref[...], preferred_element_type=F32)

    is_prompt = pl.program_id(1) < prompt_tiles
    pl.when(is_prompt)(lambda: run(yp_ref))
    pl.when(jnp.logical_not(is_prompt))(lambda: run(ys_ref))


def _matmul_residual(y_prompt, y_sample, w, widx, x, mod5, layer, tok):
    kdim = y_prompt.shape[1]
    t, d = x.shape
    tm = tok.tile((512, 256, 128))
    tn = _pick_tile(d, (1024, 512, 256, 128))
    nj = d // tn
    prompt_tiles = tok.n_prompt_tok // tm

    def gate_map(j, i):
        return (layer, tok.row(i * tm), 2, 0, j)

    return pl.pallas_call(
        functools.partial(_mmres_kernel, prompt_tiles),
        out_shape=jax.ShapeDtypeStruct((t, d), F32),
        grid=(nj, t // tm),
        in_specs=[
            pl.BlockSpec((tm, kdim), lambda j, i: (jnp.minimum(i, prompt_tiles - 1), 0)),
            pl.BlockSpec((tm, kdim), lambda j, i: (jnp.maximum(i - prompt_tiles, 0), 0)),
            pl.BlockSpec((None, kdim, tn), lambda j, i: (widx, 0, j)),
            pl.BlockSpec((tm, tn), lambda j, i: (i, j)),
            pl.BlockSpec((None, None, None, 1, tn), gate_map),
        ],
        out_specs=pl.BlockSpec((tm, tn), lambda j, i: (i, j)),
        compiler_params=_params("parallel", "parallel"),
        name="matmul_residual",
    )(y_prompt, y_sample, w, x, mod5)


def _ffn_kernel(prompt_tiles, x_ref, nw_ref, sh_ref, sc_ref, g_ref, wg_ref, wu_ref, wd_ref, *refs):
    u_ref, acc_ref = refs[-2:]
    j = pl.program_id(1)

    @pl.when(j == 0)
    def _():
        u_ref[...] = _norm_mod(x_ref, nw_ref, sh_ref, sc_ref)
        acc_ref[...] = jnp.zeros_like(acc_ref)

    u = u_ref[...]
    a = jnp.dot(u, wg_ref[...], preferred_element_type=F32)
    b = jnp.dot(u, wu_ref[...], preferred_element_type=F32)
    h = (_silu(a) * b).astype(BF16)
    acc_ref[...] += jnp.dot(h, wd_ref[...], preferred_element_type=F32)

    last = j == pl.num_programs(1) - 1
    if prompt_tiles is None:
        (o_ref,) = refs[:-2]

        @pl.when(last)
        def _():
            o_ref[...] = x_ref[...] + g_ref[...] * acc_ref[...]
    else:
        op_ref, os_ref = refs[:-2]
        is_prompt = pl.program_id(0) < prompt_tiles

        @pl.when(jnp.logical_and(last, is_prompt))
        def _():
            op_ref[...] = x_ref[...] + g_ref[...] * acc_ref[...]

        @pl.when(jnp.logical_and(last, jnp.logical_not(is_prompt)))
        def _():
            os_ref[...] = x_ref[...] + g_ref[...] * acc_ref[...]


def _ffn(x, norm_w, mod5, layer, tok, wg, wu, wd, split_out=False):
    t, d = x.shape
    hid = wg.shape[2]
    tm = tok.tile((512, 256, 128))
    th = _pick_tile(hid, (512, 256, 128))
    if split_out:
        prompt_tiles = tok.n_prompt_tok // tm
        out_shape = [jax.ShapeDtypeStruct((tok.n_prompt_tok, d), F32),
                     jax.ShapeDtypeStruct((t - tok.n_prompt_tok, d), F32)]
        out_specs = [pl.BlockSpec((tm, d), lambda i, j: (jnp.minimum(i, prompt_tiles - 1), 0)),
                     pl.BlockSpec((tm, d), lambda i, j: (jnp.maximum(i - prompt_tiles, 0), 0))]
    else:
        prompt_tiles = None
        out_shape = jax.ShapeDtypeStruct((t, d), F32)
        out_specs = pl.BlockSpec((tm, d), lambda i, j: (i, 0))
    return pl.pallas_call(
        functools.partial(_ffn_kernel, prompt_tiles),
        out_shape=out_shape,
        grid=(t // tm, hid // th),
        in_specs=[
            pl.BlockSpec((tm, d), lambda i, j: (i, 0)),
            pl.BlockSpec((1, d), lambda i, j: (0, 0)),
            _mod_spec(tok, tm, layer, 3, d, 0),
            _mod_spec(tok, tm, layer, 4, d, 0),
            _mod_spec(tok, tm, layer, 5, d, 0),
            pl.BlockSpec((None, d, th), lambda i, j: (layer, 0, j)),
            pl.BlockSpec((None, d, th), lambda i, j: (layer, 0, j)),
            pl.BlockSpec((None, th, d), lambda i, j: (layer, j, 0)),
        ],
        out_specs=out_specs,
        scratch_shapes=[pltpu.VMEM((tm, d), BF16), pltpu.VMEM((tm, d), F32)],
        compiler_params=_params("arbitrary" if split_out else "parallel", "arbitrary"),
        name="ffn",
    )(x, norm_w.reshape(1, d), mod5, mod5, mod5, wg, wu, wd)


def _ssd_dt_kernel(heads, raw_ref, bias_ref, alog_ref, dt_ref, acs_ref):
    forward = lax.broadcasted_iota(jnp.int32, (1, raw_ref.shape[1]), 1) < heads
    neg_a = -jnp.exp(alog_ref[...])
    for c in range(raw_ref.shape[0] // CHUNK):
        rows = slice(c * CHUNK, (c + 1) * CHUNK)
        dt = _softplus(raw_ref[rows, :] + bias_ref[...])
        a = dt * neg_a
        dt_ref[rows, :] = dt
        acs_ref[rows, :] = jnp.where(forward, _cumsum_chunk(a, False), _cumsum_chunk(a, True))


def _ssd_dt(dt_raw, dt_bias, a_log):
    t, cols = dt_raw.shape
    tm = _pick_tile(t, (1024, 512, 256, 128))
    return pl.pallas_call(
        functools.partial(_ssd_dt_kernel, cols // 2),
        out_shape=[jax.ShapeDtypeStruct((t, cols), F32)] * 2,
        grid=(t // tm,),
        in_specs=[
            pl.BlockSpec((tm, cols), lambda i: (i, 0)),
            pl.BlockSpec((1, cols), lambda i: (0, 0)),
            pl.BlockSpec((1, cols), lambda i: (0, 0)),
        ],
        out_specs=[pl.BlockSpec((tm, cols), lambda i: (i, 0))] * 2,
        compiler_params=_params("parallel"),
        name="ssd_dt",
    )(dt_raw, dt_bias.reshape(1, cols), a_log.reshape(1, cols))


def _ssd_kernel(nc, has_h0, has_ht, n_prev, *refs):
    (z_ref, x_ref, b_ref, c_ref, acsc_ref, dtr_ref, acsr_ref, dsk_ref, nw_ref,
     cwx_ref, cwb_ref, cwc_ref, cbx_ref, cbb_ref, cbc_ref) = refs[:15]
    pos = 15
    h0_ref = None
    ht_ref = None
    prev_ref = None
    if has_h0:
        h0_ref = refs[pos]
        pos += 1
    if n_prev:
        prev_ref = refs[pos]
        pos += 1
    y_ref = refs[pos]
    pos += 1
    if has_ht:
        ht_ref = refs[pos]
        pos += 1
    xb_scr, xt_scr, bs_scr, cs_scr, yf_scr, yb_scr, h_scr = refs[pos:]

    seq_len = nc * CHUNK
    pairs = x_ref.shape[1] // LANES
    row = lax.broadcasted_iota(jnp.int32, (CHUNK, 1), 0)
    upper = row < SSD_HEAD_DIM
    lo = lax.broadcasted_iota(jnp.int32, (1, LANES), 1) < SSD_HEAD_DIM
    ti = lax.broadcasted_iota(jnp.int32, (CHUNK, CHUNK), 0)
    si = lax.broadcasted_iota(jnp.int32, (CHUNK, CHUNK), 1)

    def conv_silu(src, w_ref, bias_ref, c, r0):
        rp = pl.multiple_of(jnp.maximum(r0 - SUBLANES, 0), SUBLANES)
        rn = pl.multiple_of(jnp.minimum(r0 + CHUNK, seq_len - SUBLANES), SUBLANES)
        cur = src[pl.ds(r0, CHUNK), :]
        prow = jnp.where(c > 0, src[pl.ds(rp, SUBLANES), :][SUBLANES - 1:SUBLANES, :], 0.0)
        nrow = jnp.where(c < nc - 1, src[pl.ds(rn, SUBLANES), :][0:1, :], 0.0)
        xm1 = jnp.where(row == 0, prow, pltpu.roll(cur, 1, 0))
        xp1 = jnp.where(row == CHUNK - 1, nrow, pltpu.roll(cur, CHUNK - 1, 0))
        w = w_ref[...]
        return _silu(w[0:1, :] * xm1 + w[1:2, :] * cur + w[2:3, :] * xp1 + bias_ref[...])

    def conv_chunk(c, carry):
        r0 = pl.multiple_of(c * CHUNK, CHUNK)
        xs = conv_silu(x_ref, cwx_ref, cbx_ref, c, r0)
        xb_scr[pl.ds(r0, CHUNK), :] = xs.astype(BF16)
        yf_scr[pl.ds(r0, CHUNK), :] = dsk_ref[...] * xs
        for p in range(pairs):
            cols = slice(LANES * p, LANES * (p + 1))
            xt_scr[cols, pl.ds(r0, CHUNK)] = xs[:, cols].T
        bs_scr[pl.ds(r0, CHUNK), :] = conv_silu(b_ref, cwb_ref, cbb_ref, c, r0).astype(BF16)
        cs_scr[pl.ds(r0, CHUNK), :] = conv_silu(c_ref, cwc_ref, cbc_ref, c, r0).astype(BF16)
        return carry

    def chunk_step(c, d):
        r0 = pl.multiple_of(c * CHUNK, CHUNK)
        xb = xb_scr[pl.ds(r0, CHUNK), :]
        bc = bs_scr[pl.ds(r0, CHUNK), :]
        cc = cs_scr[pl.ds(r0, CHUNK), :]
        acs_c = acsc_ref[pl.ds(r0, CHUNK), :]
        acs_r = acsr_ref[:, pl.ds(r0, CHUNK)]
        dt_r = dtr_ref[:, pl.ds(r0, CHUNK)]
        last = CHUNK - 1 if d == 0 else 0
        end_r = jnp.broadcast_to(acs_r[:, last:last + 1], acs_r.shape)
        w_r = dt_r * jnp.exp(end_r - acs_r)
        tot_r = jnp.exp(end_r)
        cb = lax.dot_general(cc, bc, NT_DIMS, preferred_element_type=F32)
        mask = (ti >= si) if d == 0 else (ti <= si)

        def head(k):
            col = jnp.broadcast_to(acs_c[:, k:k + 1], (CHUNK, CHUNK))
            decay = jnp.exp(jnp.where(mask, col - acs_r[k:k + 1, :], -jnp.inf)) * dt_r[k:k + 1, :]
            return col, (cb * decay).astype(BF16)

        for p in range(pairs):
            k0 = SSD_HEADS_PER_GROUP * d + 2 * p
            k1 = k0 + 1
            cols = slice(LANES * p, LANES * (p + 1))
            col0, m0 = head(k0)
            col1, m1 = head(k1)
            xbp = xb[:, cols]
            zero = jnp.zeros_like(xbp)
            y = jnp.dot(m0, jnp.where(lo, xbp, zero), preferred_element_type=F32)
            y += jnp.dot(m1, jnp.where(lo, zero, xbp), preferred_element_type=F32)
            hp = h_scr[d, cols, :]
            y_inter = lax.dot_general(cc, hp.astype(BF16), NT_DIMS, preferred_element_type=F32)
            y += y_inter * jnp.exp(jnp.where(lo, col0, col1))
            xw = xt_scr[cols, pl.ds(r0, CHUNK)] * jnp.where(upper, w_r[k0:k0 + 1, :], w_r[k1:k1 + 1, :])
            s_new = jnp.dot(xw.astype(BF16), bc, preferred_element_type=F32)
            h_scr[d, cols, :] = jnp.where(upper, tot_r[k0:k0 + 1, :], tot_r[k1:k1 + 1, :]) * hp + s_new
            if d == 0:
                yf_scr[pl.ds(r0, CHUNK), cols] += y
            else:
                yb_scr[pl.ds(r0, CHUNK), cols] = y

    if has_h0:
        h_scr[...] = h0_ref[...]
    else:
        h_scr[...] = jnp.zeros_like(h_scr)
    lax.fori_loop(0, nc, conv_chunk, 0)

    def scan_body(i, carry):
        chunk_step(i, 0)
        chunk_step(nc - 1 - i, 1)
        return carry

    lax.fori_loop(0, nc, scan_body, 0)

    def finish(c, carry):
        r0 = pl.multiple_of(c * CHUNK, CHUNK)
        y = yf_scr[pl.ds(r0, CHUNK), :] + yb_scr[pl.ds(r0, CHUNK), :]
        g = y * _silu(z_ref[pl.ds(r0, CHUNK), :])
        y_ref[pl.ds(r0, CHUNK), :] = _rms(g, nw_ref[...]).astype(y_ref.dtype)
        return carry

    lax.fori_loop(0, nc, finish, 0)
    if has_ht:
        if n_prev:
            ht_ref[0:n_prev] = prev_ref[...]
        ht_ref[n_prev] = h_scr[...]


def _ssd_scan(proj, acs_c, dt_r, acs_r, dsk, norm_w, conv_w, conv_b, row0, nseq, seq_len, h0, want_state,
              prev_states=None):
    n_prev = 0 if prev_states is None else prev_states.shape[1]
    groups = acs_c.shape[0]
    gw = SSD_HEADS_PER_GROUP * SSD_HEAD_DIM
    d_inner = groups * gw
    n = SSD_D_STATE
    nc = seq_len // CHUNK
    rb0 = row0 // seq_len
    xoff = d_inner // gw
    boff = 2 * d_inner // n
    coff = boff + groups
    cwx_off = 0
    cwb_off = d_inner // n
    cwc_off = cwb_off + groups

    in_specs = [
        pl.BlockSpec((seq_len, gw), lambda b, g: (rb0 + b, g)),
        pl.BlockSpec((seq_len, gw), lambda b, g: (rb0 + b, xoff + g)),
        pl.BlockSpec((seq_len, n), lambda b, g: (rb0 + b, boff + g)),
        pl.BlockSpec((seq_len, n), lambda b, g: (rb0 + b, coff + g)),
        pl.BlockSpec((None, seq_len, 2 * SSD_HEADS_PER_GROUP), lambda b, g: (g, rb0 + b, 0)),
        pl.BlockSpec((None, 2 * SSD_HEADS_PER_GROUP, seq_len), lambda b, g: (g, 0, rb0 + b)),
        pl.BlockSpec((None, 2 * SSD_HEADS_PER_GROUP, seq_len), lambda b, g: (g, 0, rb0 + b)),
        pl.BlockSpec((1, gw), lambda b, g: (0, g)),
        pl.BlockSpec((1, gw), lambda b, g: (0, g)),
        pl.BlockSpec((SSD_CONV_W, gw), lambda b, g: (0, cwx_off + g)),
        pl.BlockSpec((SSD_CONV_W, n), lambda b, g: (0, cwb_off + g)),
        pl.BlockSpec((SSD_CONV_W, n), lambda b, g: (0, cwc_off + g)),
        pl.BlockSpec((1, gw), lambda b, g: (0, cwx_off + g)),
        pl.BlockSpec((1, n), lambda b, g: (0, cwb_off + g)),
        pl.BlockSpec((1, n), lambda b, g: (0, cwc_off + g)),
    ]
    args = [proj, proj, proj, proj, acs_c, dt_r, acs_r, dsk, norm_w,
            conv_w, conv_w, conv_w, conv_b, conv_b, conv_b]
    if h0 is not None:
        in_specs.append(pl.BlockSpec((None, 2, gw, n), lambda b, g: (b, 0, g, 0)))
        args.append(h0)
    if n_prev:
        in_specs.append(pl.BlockSpec((None, n_prev, 2, gw, n), lambda b, g: (b, 0, 0, g, 0)))
        args.append(prev_states)
    out_shape = [jax.ShapeDtypeStruct((nseq * seq_len, d_inner), BF16)]
    out_specs = [pl.BlockSpec((seq_len, gw), lambda b, g: (b, g))]
    if want_state:
        out_shape.append(jax.ShapeDtypeStruct((nseq, n_prev + 1, 2, d_inner, n), F32))
        out_specs.append(pl.BlockSpec((None, n_prev + 1, 2, gw, n), lambda b, g: (b, 0, 0, g, 0)))
    outs = pl.pallas_call(
        functools.partial(_ssd_kernel, nc, h0 is not None, want_state, n_prev),
        out_shape=out_shape,
        grid=(nseq, groups),
        in_specs=in_specs,
        out_specs=out_specs,
        scratch_shapes=[
            pltpu.VMEM((seq_len, gw), BF16),
            pltpu.VMEM((gw, seq_len), F32),
            pltpu.VMEM((seq_len, n), BF16),
            pltpu.VMEM((seq_len, n), BF16),
            pltpu.VMEM((seq_len, gw), F32),
            pltpu.VMEM((seq_len, gw), F32),
            pltpu.VMEM((2, gw, n), F32),
        ],
        compiler_params=_params("parallel", "parallel"),
        name="ssd_scan",
    )(*args)
    return outs if want_state else (outs[0], None)


def _ssd_mixer(x, norm_w_mix, mod5, layer, tok, geo, h0_sample, prev_states, j, w_in, conv_w, conv_b, dt_bias,
               a_log, d_skip, norm_w, w_out):
    heads = dt_bias.shape[1]
    groups = heads // SSD_HEADS_PER_GROUP
    d_inner = heads * SSD_HEAD_DIM
    hpg = SSD_HEADS_PER_GROUP
    t = x.shape[0]
    proj = _norm_mod_matmul(x, norm_w_mix, mod5, layer, tok, w_in, j)
    dt_off = d_inner + conv_w.shape[1]
    dt, acs = _ssd_dt(proj[:, dt_off:dt_off + 2 * heads], dt_bias, a_log)

    def per_group(v):
        return v.reshape(t, 2, groups, hpg).transpose(2, 0, 1, 3).reshape(groups, t, 2 * hpg)

    acs_c = per_group(acs)
    acs_r = acs_c.transpose(0, 2, 1)
    dt_r = per_group(dt).transpose(0, 2, 1)
    dsk = jnp.repeat(d_skip, SSD_HEAD_DIM).reshape(1, d_inner)
    nw = norm_w.reshape(1, d_inner)
    cb = conv_b.reshape(1, -1)
    bp, lp, bs, ls = geo
    y_p, states = _ssd_scan(proj, acs_c, dt_r, acs_r, dsk, nw, conv_w, cb, 0, bp, lp, None, True, prev_states)
    h0 = h0_sample.reshape(bs, 2, d_inner, SSD_D_STATE)
    y_s, _ = _ssd_scan(proj, acs_c, dt_r, acs_r, dsk, nw, conv_w, cb, bp * lp, bs, ls, h0, False)
    x = _matmul_residual(y_p, y_s, w_out, j, x, mod5, layer, tok)
    return x, states


def _rope(x, cos, sin):
    quarter = ATTN_HEAD_DIM // 4
    lane = lax.broadcasted_iota(jnp.int32, (1, ATTN_HEAD_DIM), 1)
    first = (lane % (2 * quarter)) < quarter
    swapped = jnp.where(first, pltpu.roll(x, ATTN_HEAD_DIM - quarter, 1), pltpu.roll(x, quarter, 1))
    return x * cos + swapped * sin


def _attn_kernel(nq, has_cache, use_rope, write_k, *refs):
    q_ref, k_ref, v_ref, qn_ref, kn_ref = refs[:5]
    pos = 5
    cos_ref = sin_ref = kc_ref = vc_ref = knew_ref = None
    if use_rope:
        cos_ref, sin_ref = refs[pos:pos + 2]
        pos += 2
    if has_cache:
        kc_ref, vc_ref = refs[pos:pos + 2]
        pos += 2
    o_ref = refs[pos]
    pos += 1
    if write_k:
        knew_ref = refs[pos]
        pos += 1
    kb_scr, vt_scr = refs[pos:pos + 2]
    pos += 2
    if has_cache:
        kcb_scr, vct_scr = refs[pos:pos + 2]
        kcb_scr[...] = kc_ref[...].astype(BF16)
        for c in range(vc_ref.shape[0] // CHUNK):
            rows = slice(c * CHUNK, (c + 1) * CHUNK)
            vct_scr[:, rows] = vc_ref[rows, :].T.astype(BF16)

    grp = q_ref.shape[1] // ATTN_HEAD_DIM
    scale = ATTN_HEAD_DIM ** -0.5

    def kprep(c, carry):
        r0 = pl.multiple_of(c * CHUNK, CHUNK)
        kn = _rms(k_ref[pl.ds(r0, CHUNK), :], kn_ref[...])
        if write_k:
            knew_ref[pl.ds(r0, CHUNK), :] = kn
        if use_rope:
            kn = _rope(kn, cos_ref[pl.ds(r0, CHUNK), :], sin_ref[pl.ds(r0, CHUNK), :])
        kb_scr[pl.ds(r0, CHUNK), :] = kn.astype(BF16)
        vt_scr[:, pl.ds(r0, CHUNK)] = v_ref[pl.ds(r0, CHUNK), :].T.astype(BF16)
        return carry

    lax.fori_loop(0, nq, kprep, 0)

    def qblock(c, carry):
        r0 = pl.multiple_of(c * CHUNK, CHUNK)
        q_t = []
        for g in range(grp):
            cols = slice(ATTN_HEAD_DIM * g, ATTN_HEAD_DIM * (g + 1))
            qn = _rms(q_ref[pl.ds(r0, CHUNK), cols], qn_ref[...])
            if use_rope:
                qn = _rope(qn, cos_ref[pl.ds(r0, CHUNK), :], sin_ref[pl.ds(r0, CHUNK), :])
            q_t.append((qn * (scale * LOG2_E)).T.astype(BF16))
        q_t = jnp.concatenate(q_t, axis=1)
        s = jnp.dot(kb_scr[...], q_t, preferred_element_type=F32)
        m = jnp.max(s, axis=0, keepdims=True)
        if has_cache:
            sc = jnp.dot(kcb_scr[...], q_t, preferred_element_type=F32)
            m = jnp.maximum(m, jnp.max(sc, axis=0, keepdims=True))
            ec = jnp.exp2(sc - m)
        e = jnp.exp2(s - m)
        den = jnp.sum(e, axis=0, keepdims=True)
        if has_cache:
            den = den + jnp.sum(ec, axis=0, keepdims=True)
        rcp = 1.0 / den
        o_t = jnp.dot(vt_scr[...], (e * rcp).astype(BF16), preferred_element_type=F32)
        if has_cache:
            o_t += jnp.dot(vct_scr[...], (ec * rcp).astype(BF16), preferred_element_type=F32)
        for g in range(grp):
            cols = slice(ATTN_HEAD_DIM * g, ATTN_HEAD_DIM * (g + 1))
            o_ref[pl.ds(r0, CHUNK), cols] = o_t[:, cols].T.astype(o_ref.dtype)
        return carry

    lax.fori_loop(0, nq, qblock, 0)


def _attention(qkv, q_norm, k_norm, heads, kv_heads, row0, nseq, seq_len, rope, cache, write_k):
    dh = ATTN_HEAD_DIM
    grp = heads // kv_heads
    rb0 = row0 // seq_len
    nq = seq_len // CHUNK
    in_specs = [
        pl.BlockSpec((seq_len, grp * dh), lambda b, h: (rb0 + b, h)),
        pl.BlockSpec((seq_len, dh), lambda b, h: (rb0 + b, heads + h)),
        pl.BlockSpec((seq_len, dh), lambda b, h: (rb0 + b, heads + kv_heads + h)),
        pl.BlockSpec((1, dh), lambda b, h: (0, 0)),
        pl.BlockSpec((1, dh), lambda b, h: (0, 0)),
    ]
    args = [qkv, qkv, qkv, q_norm.reshape(1, dh), k_norm.reshape(1, dh)]
    scratch = [pltpu.VMEM((seq_len, dh), BF16), pltpu.VMEM((dh, seq_len), BF16)]
    if rope is not None:
        in_specs += [pl.BlockSpec((seq_len, dh), lambda b, h: (0, 0))] * 2
        args += list(rope)
    if cache is not None:
        past = cache[0].shape[1]
        in_specs += [pl.BlockSpec((None, past, dh), lambda b, h: (b, 0, h))] * 2
        args += list(cache)
        scratch += [pltpu.VMEM((past, dh), BF16), pltpu.VMEM((dh, past), BF16)]
    out_shape = [jax.ShapeDtypeStruct((nseq * seq_len, heads * dh), BF16)]
    out_specs = [pl.BlockSpec((seq_len, grp * dh), lambda b, h: (b, h))]
    if write_k:
        out_shape.append(jax.ShapeDtypeStruct((nseq * seq_len, kv_heads * dh), F32))
        out_specs.append(pl.BlockSpec((seq_len, dh), lambda b, h: (b, h)))
    outs = pl.pallas_call(
        functools.partial(_attn_kernel, nq, cache is not None, rope is not None, write_k),
        out_shape=out_shape,
        grid=(nseq, kv_heads),
        in_specs=in_specs,
        out_specs=out_specs,
        scratch_shapes=scratch,
        compiler_params=_params("parallel", "parallel"),
        name="attention",
    )(*args)
    return outs if write_k else (outs[0], None)


def _rope_tables(seq_len):
    quarter = ATTN_HEAD_DIM // 4
    t = jnp.arange(seq_len)
    lane = jnp.arange(ATTN_HEAD_DIM)
    pos = jnp.where(lane[None, :] < 2 * quarter, (t // GRID_W)[:, None], (t % GRID_W)[:, None]).astype(F32)
    inv_freq = ROPE_THETA ** (-jnp.arange(quarter, dtype=F32) / quarter)
    ang = pos * inv_freq[lane % quarter][None, :]
    sign = jnp.where((lane % (2 * quarter)) < quarter, -1.0, 1.0).astype(F32)
    return jnp.cos(ang), jnp.sin(ang) * sign[None, :]


def _attn_mixer(x, norm_w_mix, mod5, layer, tok, geo, cache_k, cache_v, j, w_qkv, q_norm, k_norm, w_out):
    bp, lp, bs, ls = geo
    dh = ATTN_HEAD_DIM
    kv_heads = cache_k.shape[2]
    heads = w_out.shape[1] // dh
    qkv = _norm_mod_matmul(x, norm_w_mix, mod5, layer, tok, w_qkv, j)
    o_p, k_new = _attention(qkv, q_norm, k_norm, heads, kv_heads, 0, bp, lp, None, None, True)
    past = cache_k.shape[1]
    cache = (cache_k.reshape(bs, past, kv_heads * dh), cache_v.reshape(bs, past, kv_heads * dh))
    o_s, _ = _attention(qkv, q_norm, k_norm, heads, kv_heads, bp * lp, bs, ls, _rope_tables(ls), cache, False)
    x = _matmul_residual(o_p, o_s, w_out, j, x, mod5, layer, tok)
    v_new = qkv[:bp * lp, (heads + kv_heads) * dh:]
    return x, k_new.reshape(bp, lp, kv_heads, dh), v_new.reshape(bp, lp, kv_heads, dh)


def _mlstm_gate_kernel(heads, i_ref, f_ref, bi_ref, bf_ref, b_ref, r_ref, p_ref):
    forward = lax.broadcasted_iota(jnp.int32, (1, i_ref.shape[1]), 1) < heads
    for c in range(i_ref.shape[0] // CHUNK):
        rows = slice(c * CHUNK, (c + 1) * CHUNK)
        log_f = _log_sigmoid(f_ref[rows, :] + bf_ref[...])
        b = jnp.where(forward, _cumsum_chunk(log_f, False), _cumsum_chunk(log_f, True))
        r = i_ref[rows, :] + bi_ref[...] - b
        b_ref[rows, :] = b
        r_ref[rows, :] = r
        p_ref[rows, :] = jnp.where(forward, _cummax_chunk(r, False), _cummax_chunk(r, True))


def _mlstm_gates(i_raw, f_raw, b_i, b_f):
    t, cols = i_raw.shape
    tm = _pick_tile(t, (1024, 512, 256, 128))
    return pl.pallas_call(
        functools.partial(_mlstm_gate_kernel, cols // 2),
        out_shape=[jax.ShapeDtypeStruct((t, cols), F32)] * 3,
        grid=(t // tm,),
        in_specs=[
            pl.BlockSpec((tm, cols), lambda i: (i, 0)),
            pl.BlockSpec((tm, cols), lambda i: (i, 0)),
            pl.BlockSpec((1, cols), lambda i: (0, 0)),
            pl.BlockSpec((1, cols), lambda i: (0, 0)),
        ],
        out_specs=[pl.BlockSpec((tm, cols), lambda i: (i, 0))] * 3,
        compiler_params=_params("parallel"),
        name="mlstm_gates",
    )(i_raw, f_raw, b_i, b_f)


def _mlstm_kernel(nc, has_s0, has_st, *refs):
    q_ref, k_ref, v_ref, og_ref, cp_ref, rr_ref, hn_ref = refs[:7]
    pos = 7
    c0_ref = n0_ref = m0_ref = ct_ref = nt_ref = mt_ref = None
    if has_s0:
        c0_ref, n0_ref, m0_ref = refs[pos:pos + 3]
        pos += 3
    h_ref = refs[pos]
    pos += 1
    if has_st:
        ct_ref, nt_ref, mt_ref = refs[pos:pos + 3]
        pos += 3
    hf_scr, hb_scr, c_scr, n_scr, m_scr = refs[pos:]

    dk = q_ref.shape[1]
    dv = v_ref.shape[1]
    qscale = dk ** -0.5
    tile = (CHUNK, CHUNK)
    ti = lax.broadcasted_iota(jnp.int32, tile, 0)
    si = lax.broadcasted_iota(jnp.int32, tile, 1)

    def wide(x):
        return jnp.concatenate([x] * (dv // CHUNK), axis=1)

    def chunk_step(c, d):
        r0 = pl.multiple_of(c * CHUNK, CHUNK)
        q = q_ref[pl.ds(r0, CHUNK), :] * qscale
        k = k_ref[pl.ds(r0, CHUNK), :]
        qb = q.astype(BF16)
        kb = k.astype(BF16)
        vb = v_ref[pl.ds(r0, CHUNK), :].astype(BF16)
        cp = cp_ref[pl.ds(r0, CHUNK), :]
        b_t = jnp.broadcast_to(cp[:, 3 * d:3 * d + 1], tile)
        r_t = jnp.broadcast_to(cp[:, 3 * d + 1:3 * d + 2], tile)
        p_t = jnp.broadcast_to(cp[:, 3 * d + 2:3 * d + 3], tile)
        r_row = rr_ref[d:d + 1, pl.ds(r0, CHUNK)]
        m_row = m_scr[d]
        u_t = jnp.maximum(m_row, p_t)
        mask = (ti >= si) if d == 0 else (ti <= si)
        s = lax.dot_general(qb, kb, NT_DIMS, preferred_element_type=F32)
        s = s * jnp.exp(jnp.where(mask, r_row - u_t, -jnp.inf))
        inter = jnp.exp(m_row - u_t)
        cst = c_scr[d]
        nst = n_scr[d]
        num = jnp.dot(s.astype(BF16), vb, preferred_element_type=F32)
        num += wide(inter) * jnp.dot(qb, cst.astype(BF16), preferred_element_type=F32)
        den = jnp.sum(s + inter * (q * nst), axis=1, keepdims=True)
        rcp = 1.0 / jnp.maximum(jnp.abs(den), jnp.exp(-(b_t + u_t)))
        h = num * wide(rcp)
        last = CHUNK - 1 if d == 0 else 0
        u_last = u_t[last:last + 1, :]
        wk = jnp.exp(r_t - u_last) * k
        carry_decay = jnp.exp(m_row - u_last)
        c_scr[d] = wide(carry_decay) * cst + jnp.dot(wk.T.astype(BF16), vb, preferred_element_type=F32)
        n_scr[d] = carry_decay * nst + jnp.sum(wk, axis=0, keepdims=True)
        m_scr[d] = b_t[last:last + 1, :] + u_last
        (hf_scr if d == 0 else hb_scr)[pl.ds(r0, CHUNK), :] = h

    for d in range(2):
        if has_s0:
            c_scr[d] = c0_ref[d]
            n_scr[d] = n0_ref[d]
            m_scr[d] = m0_ref[d]
        else:
            c_scr[d] = jnp.zeros(c_scr.shape[1:], F32)
            n_scr[d] = jnp.zeros(n_scr.shape[1:], F32)
            m_scr[d] = jnp.zeros(m_scr.shape[1:], F32)

    def scan_body(i, carry):
        chunk_step(i, 0)
        chunk_step(nc - 1 - i, 1)
        return carry

    lax.fori_loop(0, nc, scan_body, 0)

    def finish(c, carry):
        r0 = pl.multiple_of(c * CHUNK, CHUNK)
        hs = _rms(hf_scr[pl.ds(r0, CHUNK), :] + hb_scr[pl.ds(r0, CHUNK), :], hn_ref[...])
        h_ref[pl.ds(r0, CHUNK), :] = (hs * _sigmoid(og_ref[pl.ds(r0, CHUNK), :])).astype(h_ref.dtype)
        return carry

    lax.fori_loop(0, nc, finish, 0)
    if has_st:
        for d in range(2):
            ct_ref[d] = c_scr[d]
            nt_ref[d] = n_scr[d]
            mt_ref[d] = m_scr[d]


def _mlstm_scan(proj, gate_cols, gate_rows, head_norm, heads, dk, dv, row0, nseq, seq_len, state, want_state):
    assert dv == 2 * dk and dk == CHUNK
    nc = seq_len // CHUNK
    rb0 = row0 // seq_len
    in_specs = [
        pl.BlockSpec((seq_len, dk), lambda b, h: (rb0 + b, h)),
        pl.BlockSpec((seq_len, dk), lambda b, h: (rb0 + b, heads + h)),
        pl.BlockSpec((seq_len, dv), lambda b, h: (rb0 + b, heads + h)),
        pl.BlockSpec((seq_len, dv), lambda b, h: (rb0 + b, 2 * heads + h)),
        pl.BlockSpec((None, seq_len, 6), lambda b, h: (h, rb0 + b, 0)),
        pl.BlockSpec((None, 2, seq_len), lambda b, h: (h, 0, rb0 + b)),
        pl.BlockSpec((None, 1, dv), lambda b, h: (h, 0, 0)),
    ]
    args = [proj, proj, proj, proj, gate_cols, gate_rows, head_norm]
    state_specs = [
        pl.BlockSpec((None, 2, None, dk, dv), lambda b, h: (b, 0, h, 0, 0)),
        pl.BlockSpec((None, 2, None, 1, dk), lambda b, h: (b, 0, h, 0, 0)),
        pl.BlockSpec((None, 2, None, 1, LANES), lambda b, h: (b, 0, h, 0, 0)),
    ]
    if state is not None:
        in_specs += state_specs
        args += list(state)
    out_shape = [jax.ShapeDtypeStruct((nseq * seq_len, heads * dv), BF16)]
    out_specs = [pl.BlockSpec((seq_len, dv), lambda b, h: (b, h))]
    if want_state:
        out_shape += [jax.ShapeDtypeStruct((nseq, 2, heads, dk, dv), F32),
                      jax.ShapeDtypeStruct((nseq, 2, heads, 1, dk), F32),
                      jax.ShapeDtypeStruct((nseq, 2, heads, 1, LANES), F32)]
        out_specs += state_specs
    outs = pl.pallas_call(
        functools.partial(_mlstm_kernel, nc, state is not None, want_state),
        out_shape=out_shape,
        grid=(nseq, heads),
        in_specs=in_specs,
        out_specs=out_specs,
        scratch_shapes=[
            pltpu.VMEM((seq_len, dv), F32),
            pltpu.VMEM((seq_len, dv), F32),
            pltpu.VMEM((2, dk, dv), F32),
            pltpu.VMEM((2, 1, dk), F32),
            pltpu.VMEM((2, 1, LANES), F32),
        ],
        compiler_params=_params("parallel", "parallel"),
        name="mlstm_scan",
    )(*args)
    return outs


def _mlstm_mixer(x, norm_w_mix, mod5, layer, tok, geo, c0, n0, m0, j, w_in, b_gates, head_norm, w_out):
    bp, lp, bs, ls = geo
    heads, dk, dv = c0.shape[2], c0.shape[3], c0.shape[4]
    t = x.shape[0]
    proj = _norm_mod_matmul(x, norm_w_mix, mod5, layer, tok, w_in, j)
    g_off = 2 * heads * dk + 2 * heads * dv
    graw = proj[:, g_off:g_off + 4 * heads].reshape(t, 2, 2, heads)
    b, r, p = _mlstm_gates(graw[:, :, 0, :].reshape(t, 2 * heads), graw[:, :, 1, :].reshape(t, 2 * heads),
                           b_gates[:, 0, :].reshape(1, 2 * heads), b_gates[:, 1, :].reshape(1, 2 * heads))
    gate_cols = jnp.stack([b, r, p], axis=-1).reshape(t, 2, heads, 3).transpose(2, 0, 1, 3).reshape(heads, t, 6)
    gate_rows = r.reshape(t, 2, heads).transpose(2, 1, 0)
    hn = head_norm.reshape(heads, 1, dv)
    outs_p = _mlstm_scan(proj, gate_cols, gate_rows, hn, heads, dk, dv, 0, bp, lp, None, True)
    h_p, c_t, n_t, m_t = outs_p
    state = (c0, n0.reshape(bs, 2, heads, 1, dk),
             jnp.broadcast_to(m0[..., None, None], (bs, 2, heads, 1, LANES)))
    (h_s,) = _mlstm_scan(proj, gate_cols, gate_rows, hn, heads, dk, dv, bp * lp, bs, ls, state, False)
    x = _matmul_residual(h_p, h_s, w_out, j, x, mod5, layer, tok)
    return x, c_t, n_t.reshape(bp, 2, heads, dk), m_t[:, :, :, 0, 0]


def kernel(x_prompt, x_sample, c, state_ssd, cache_attn_k, cache_attn_v, state_mlstm_C, state_mlstm_n, state_mlstm_m, c_ctx, ada_w, ada_b, norm_mix_w, norm_ffn_w, ffn_w_gate, ffn_w_up, ffn_w_down, ssd_w_in, ssd_conv_w, ssd_conv_b, ssd_dt_bias, ssd_a_log, ssd_d, ssd_norm_w, ssd_w_out, attn_w_qkv, attn_q_norm, attn_k_norm, attn_w_out, mlstm_w_in, mlstm_b_gates, mlstm_head_norm, mlstm_w_out):
    bp, lp, d = x_prompt.shape
    bs, ls, _ = x_sample.shape
    depth = ada_w.shape[0]
    geo = (bp, lp, bs, ls)
    tok = _Tokens(bp * lp, ls)
    x = jnp.concatenate([x_prompt.reshape(bp * lp, d), x_sample.reshape(bs * ls, d)], axis=0)

    mod_rows = -(-(1 + bs) // SUBLANES) * SUBLANES
    cc = jnp.concatenate([c_ctx[None, :], c, jnp.zeros((mod_rows - 1 - bs, d), F32)], axis=0)
    mod5 = _modulation(cc, ada_w, ada_b).reshape(depth, mod_rows, 6, 1, d)

    w_gate, w_up, w_down = _to_bf16(ffn_w_gate), _to_bf16(ffn_w_up), _to_bf16(ffn_w_down)
    ssd_in, ssd_out = _to_bf16(ssd_w_in), _to_bf16(ssd_w_out)
    attn_qkv, attn_out = _to_bf16(attn_w_qkv), _to_bf16(attn_w_out)
    mlstm_cols = mlstm_w_in.shape[2]
    mlstm_in = jnp.pad(_to_bf16(mlstm_w_in), ((0, 0), (0, 0), (0, -mlstm_cols % LANES)))
    mlstm_out = _to_bf16(mlstm_w_out)
    ssd_states = None
    new_k, new_v, new_c, new_n, new_m = [], [], [], [], []
    for l in range(depth):
        kind, j = l % 3, l // 3
        if kind == 0:
            x, ssd_states = _ssd_mixer(x, norm_mix_w[l], mod5, l, tok, geo, state_ssd[:, j], ssd_states, j, ssd_in,
                                       ssd_conv_w[j], ssd_conv_b[j], ssd_dt_bias[j], ssd_a_log[j], ssd_d[j],
                                       ssd_norm_w[j], ssd_out)
        elif kind == 1:
            x, k_new, v_new = _attn_mixer(x, norm_mix_w[l], mod5, l, tok, geo, cache_attn_k[:, j], cache_attn_v[:, j],
                                          j, attn_qkv, attn_q_norm[j], attn_k_norm[j], attn_out)
            new_k.append(k_new)
            new_v.append(v_new)
        else:
            x, c_t, n_t, m_t = _mlstm_mixer(x, norm_mix_w[l], mod5, l, tok, geo, state_mlstm_C[:, j],
                                            state_mlstm_n[:, j], state_mlstm_m[:, j], j, mlstm_in,
                                            mlstm_b_gates[j], mlstm_head_norm[j], mlstm_out)
            new_c.append(c_t)
            new_n.append(n_t)
            new_m.append(m_t)
        x = _ffn(x, norm_ffn_w[l], mod5, l, tok, w_gate, w_up, w_down, split_out=(l == depth - 1))

    y_prompt = x[0].reshape(bp, lp, d)
    y_sample = x[1].reshape(bs, ls, d)
    ssd_states = ssd_states.reshape((bp,) + state_ssd.shape[1:])
    return (y_prompt, y_sample, ssd_states, jnp.stack(new_k, axis=1), jnp.stack(new_v, axis=1),
            jnp.stack(new_c, axis=1), jnp.stack(new_n, axis=1), jnp.stack(new_m, axis=1))
```

```python
import functools

import jax
import jax.numpy as jnp
from jax import lax
from jax.experimental import pallas as pl
from jax.experimental.pallas import tpu as pltpu

F32 = jnp.float32
BF16 = jnp.bfloat16

EPS = 1e-6
CHUNK = 128
GRID_W = 64
ROPE_THETA = 10000.0
SSD_HEAD_DIM = 64
SSD_D_STATE = 128
SSD_CONV_W = 3
SSD_HEADS_PER_GROUP = 8
ATTN_HEAD_DIM = 128
LANES = 128
SUBLANES = 8
MXU_COLS = 256
VMEM_LIMIT_BYTES = 56 * 1024 * 1024

NT_DIMS = (((1,), (1,)), ((), ()))
LOG2_E = 1.4426950408889634


def _sigmoid(x):
    return 0.5 + 0.5 * jnp.tanh(0.5 * x)


def _silu(x):
    return x * _sigmoid(x)


def _softplus(x):
    return jnp.maximum(x, 0.0) + jnp.log1p(jnp.exp(-jnp.abs(x)))


def _log_sigmoid(x):
    return -_softplus(-x)


def _rms(x, w):
    return x * lax.rsqrt(jnp.mean(x * x, axis=-1, keepdims=True) + EPS) * w


def _params(*sem):
    return pltpu.CompilerParams(dimension_semantics=sem, vmem_limit_bytes=VMEM_LIMIT_BYTES)


def _pick_tile(n, candidates):
    for c in candidates:
        if n % c == 0:
            return c
    return n


def _scan_chunk(a, reverse, combine, identity):
    idx = lax.broadcasted_iota(jnp.int32, (CHUNK, 1), 0)
    sh = 1
    while sh < CHUNK:
        if reverse:
            a = combine(a, jnp.where(idx < CHUNK - sh, pltpu.roll(a, CHUNK - sh, 0), identity))
        else:
            a = combine(a, jnp.where(idx >= sh, pltpu.roll(a, sh, 0), identity))
        sh *= 2
    return a


def _cumsum_chunk(a, reverse):
    return _scan_chunk(a, reverse, jnp.add, 0.0)


def _cummax_chunk(a, reverse):
    return _scan_chunk(a, reverse, jnp.maximum, -jnp.inf)


def _cast_kernel(w_ref, o_ref):
    n = w_ref.shape[1]
    o_ref[:, :n] = w_ref[...].astype(BF16)
    if o_ref.shape[1] > n:
        o_ref[:, n:] = jnp.zeros((o_ref.shape[0], o_ref.shape[1] - n), BF16)


def _to_bf16(w, col_multiple=1):
    shape = w.shape
    w2 = w.reshape(-1, shape[-1])
    rows, n = w2.shape
    n_out = -(-n // col_multiple) * col_multiple
    if n_out != n and n % LANES:
        return jnp.pad(_to_bf16(w), [(0, 0)] * (len(shape) - 1) + [(0, n_out - n)])
    tr = _pick_tile(rows, (256, 128, 64, 32, 16))
    out = pl.pallas_call(
        _cast_kernel,
        out_shape=jax.ShapeDtypeStruct((rows, n_out), BF16),
        grid=(rows // tr,),
        in_specs=[pl.BlockSpec((tr, n), lambda i: (i, 0))],
        out_specs=pl.BlockSpec((tr, n_out), lambda i: (i, 0)),
        compiler_params=_params("parallel"),
        name="cast_bf16",
    )(w2)
    return out.reshape(shape[:-1] + (n_out,))


def _mod_kernel(c_ref, w_ref, b_ref, o_ref):
    s = _silu(c_ref[...]).astype(BF16)
    o_ref[...] = jnp.dot(s, w_ref[...].astype(BF16), preferred_element_type=F32) + b_ref[...]


def _modulation(cc, ada_w, ada_b):
    depth, d, n = ada_w.shape
    rows = cc.shape[0]
    tn = _pick_tile(n, (1024, 512, 256, 128))
    return pl.pallas_call(
        _mod_kernel,
        out_shape=jax.ShapeDtypeStruct((depth, rows, n), F32),
        grid=(depth, n // tn),
        in_specs=[
            pl.BlockSpec((rows, d), lambda l, j: (0, 0)),
            pl.BlockSpec((None, d, tn), lambda l, j: (l, 0, j)),
            pl.BlockSpec((None, 1, tn), lambda l, j: (l, 0, j)),
        ],
        out_specs=pl.BlockSpec((None, rows, tn), lambda l, j: (l, 0, j)),
        compiler_params=_params("parallel", "parallel"),
        name="adaln_mod",
    )(cc, ada_w, ada_b.reshape(depth, 1, n))


class _Tokens:
    def __init__(self, n_prompt_tok, dec_seq):
        self.n_prompt_tok = n_prompt_tok
        self.dec_seq = dec_seq

    def row(self, start):
        return jnp.where(start < self.n_prompt_tok, 0, 1 + (start - self.n_prompt_tok) // self.dec_seq)

    def tile(self, candidates):
        for c in candidates:
            if self.n_prompt_tok % c == 0 and self.dec_seq % c == 0:
                return c
        raise ValueError("no token tile fits the sequence layout")


def _mod_spec(tok, tm, layer, k, d, grid_pos):
    def index_map(*ids):
        return (layer, tok.row(ids[grid_pos] * tm), k, 0, 0)
    return pl.BlockSpec((None, None, None, 1, d), index_map)


def _norm_mod(x_ref, nw_ref, sh_ref, sc_ref):
    return (_rms(x_ref[...], nw_ref[...]) * (1.0 + sc_ref[...]) + sh_ref[...]).astype(BF16)


def _nmm_kernel(x_ref, nw_ref, sh_ref, sc_ref, w_ref, o_ref, u_ref):
    @pl.when(pl.program_id(1) == 0)
    def _():
        u_ref[...] = _norm_mod(x_ref, nw_ref, sh_ref, sc_ref)

    o_ref[...] = jnp.dot(u_ref[...], w_ref[...], preferred_element_type=F32)


def _norm_mod_matmul(x, norm_w, mod5, layer, tok, w, widx):
    t, d = x.shape
    n = w.shape[2]
    tm = tok.tile((1024, 512, 256, 128))
    tn = _pick_tile(n, (1536, 1280, 1024, 768, 512, 256, 128))
    return pl.pallas_call(
        _nmm_kernel,
        out_shape=jax.ShapeDtypeStruct((t, n), F32),
        grid=(t // tm, n // tn),
        in_specs=[
            pl.BlockSpec((tm, d), lambda i, j: (i, 0)),
            pl.BlockSpec((1, d), lambda i, j: (0, 0)),
            _mod_spec(tok, tm, layer, 0, d, 0),
            _mod_spec(tok, tm, layer, 1, d, 0),
            pl.BlockSpec((None, d, tn), lambda i, j: (widx, 0, j)),
        ],
        out_specs=pl.BlockSpec((tm, tn), lambda i, j: (i, j)),
        scratch_shapes=[pltpu.VMEM((tm, d), BF16)],
        compiler_params=_params("parallel", "arbitrary"),
        name="norm_mod_matmul",
    )(x, norm_w.reshape(1, d), mod5, mod5, w)


def _mmres_kernel(prompt_tiles, yp_ref, ys_ref, w_ref, x_ref, g_ref, o_ref):
    def run(y_ref):
        o_ref[...] = x_ref[...] + g_ref[...] * jnp.dot(y_ref[...], w_ref[...], preferred_element_type=F32)

    is_prompt = pl.program_id(1) < prompt_tiles
    pl.when(is_prompt)(lambda: run(yp_ref))
    pl.when(jnp.logical_not(is_prompt))(lambda: run(ys_ref))


def _matmul_residual(y_prompt, y_sample, w, widx, x, mod5, layer, tok):
    kdim = y_prompt.shape[1]
    t, d = x.shape
    tm = tok.tile((512, 256, 128))
    tn = _pick_tile(d, (1024, 512, 256, 128))
    nj = d // tn
    prompt_tiles = tok.n_prompt_tok // tm

    def gate_map(j, i):
        return (layer, tok.row(i * tm), 2, 0, j)

    return pl.pallas_call(
        functools.partial(_mmres_kernel, prompt_tiles),
        out_shape=jax.ShapeDtypeStruct((t, d), F32),
        grid=(nj, t // tm),
        in_specs=[
            pl.BlockSpec((tm, kdim), lambda j, i: (jnp.minimum(i, prompt_tiles - 1), 0)),
            pl.BlockSpec((tm, kdim), lambda j, i: (jnp.maximum(i - prompt_tiles, 0), 0)),
            pl.BlockSpec((None, kdim, tn), lambda j, i: (widx, 0, j)),
            pl.BlockSpec((tm, tn), lambda j, i: (i, j)),
            pl.BlockSpec((None, None, None, 1, tn), gate_map),
        ],
        out_specs=pl.BlockSpec((tm, tn), lambda j, i: (i, j)),
        compiler_params=_params("parallel", "parallel"),
        name="matmul_residual",
    )(y_prompt, y_sample, w, x, mod5)


def _ffn_kernel(prompt_tiles, x_ref, nw_ref, sh_ref, sc_ref, g_ref, wg_ref, wu_ref, wd_ref, *refs):
    u_ref, acc_ref = refs[-2:]
    j = pl.program_id(1)

    @pl.when(j == 0)
    def _():
        u_ref[...] = _norm_mod(x_ref, nw_ref, sh_ref, sc_ref)
        acc_ref[...] = jnp.zeros_like(acc_ref)

    u = u_ref[...]
    a = jnp.dot(u, wg_ref[...], preferred_element_type=F32)
    b = jnp.dot(u, wu_ref[...], preferred_element_type=F32)
    h = (_silu(a) * b).astype(BF16)
    acc_ref[...] += jnp.dot(h, wd_ref[...], preferred_element_type=F32)

    last = j == pl.num_programs(1) - 1
    if prompt_tiles is None:
        (o_ref,) = refs[:-2]

        @pl.when(last)
        def _():
            o_ref[...] = x_ref[...] + g_ref[...] * acc_ref[...]
    else:
        op_ref, os_ref = refs[:-2]
        is_prompt = pl.program_id(0) < prompt_tiles

        @pl.when(jnp.logical_and(last, is_prompt))
        def _():
            op_ref[...] = x_ref[...] + g_ref[...] * acc_ref[...]

        @pl.when(jnp.logical_and(last, jnp.logical_not(is_prompt)))
        def _():
            os_ref[...] = x_ref[...] + g_ref[...] * acc_ref[...]


def _ffn(x, norm_w, mod5, layer, tok, wg, wu, wd, split_out=False):
    t, d = x.shape
    hid = wg.shape[2]
    tm = tok.tile((512, 256, 128))
    th = _pick_tile(hid, (512, 256, 128))
    if split_out:
        prompt_tiles = tok.n_prompt_tok // tm
        out_shape = [jax.ShapeDtypeStruct((tok.n_prompt_tok, d), F32),
                     jax.ShapeDtypeStruct((t - tok.n_prompt_tok, d), F32)]
        out_specs = [pl.BlockSpec((tm, d), lambda i, j: (jnp.minimum(i, prompt_tiles - 1), 0)),
                     pl.BlockSpec((tm, d), lambda i, j: (jnp.maximum(i - prompt_tiles, 0), 0))]
    else:
        prompt_tiles = None
        out_shape = jax.ShapeDtypeStruct((t, d), F32)
        out_specs = pl.BlockSpec((tm, d), lambda i, j: (i, 0))
    return pl.pallas_call(
        functools.partial(_ffn_kernel, prompt_tiles),
        out_shape=out_shape,
        grid=(t // tm, hid // th),
        in_specs=[
            pl.BlockSpec((tm, d), lambda i, j: (i, 0)),
            pl.BlockSpec((1, d), lambda i, j: (0, 0)),
            _mod_spec(tok, tm, layer, 3, d, 0),
            _mod_spec(tok, tm, layer, 4, d, 0),
            _mod_spec(tok, tm, layer, 5, d, 0),
            pl.BlockSpec((None, d, th), lambda i, j: (layer, 0, j)),
            pl.BlockSpec((None, d, th), lambda i, j: (layer, 0, j)),
            pl.BlockSpec((None, th, d), lambda i, j: (layer, j, 0)),
        ],
        out_specs=out_specs,
        scratch_shapes=[pltpu.VMEM((tm, d), BF16), pltpu.VMEM((tm, d), F32)],
        compiler_params=_params("arbitrary" if split_out else "parallel", "arbitrary"),
        name="ffn",
    )(x, norm_w.reshape(1, d), mod5, mod5, mod5, wg, wu, wd)


def _ssd_dt_kernel(heads, raw_ref, bias_ref, alog_ref, ldt_ref, acs_ref):
    forward = lax.broadcasted_iota(jnp.int32, (1, raw_ref.shape[1]), 1) < heads
    neg_a = -jnp.exp(alog_ref[...])
    for c in range(raw_ref.shape[0] // CHUNK):
        rows = slice(c * CHUNK, (c + 1) * CHUNK)
        dt = _softplus(raw_ref[rows, :] + bias_ref[...])
        a = dt * neg_a
        acs2 = jnp.where(forward, _cumsum_chunk(a, False), _cumsum_chunk(a, True)) * LOG2_E
        acs_ref[rows, :] = acs2
        ldt_ref[rows, :] = jnp.log(dt) * LOG2_E - acs2


def _ssd_dt(dt_raw, dt_bias, a_log):
    t, cols = dt_raw.shape
    tm = _pick_tile(t, (1024, 512, 256, 128))
    return pl.pallas_call(
        functools.partial(_ssd_dt_kernel, cols // 2),
        out_shape=[jax.ShapeDtypeStruct((t, cols), F32)] * 2,
        grid=(t // tm,),
        in_specs=[
            pl.BlockSpec((tm, cols), lambda i: (i, 0)),
            pl.BlockSpec((1, cols), lambda i: (0, 0)),
            pl.BlockSpec((1, cols), lambda i: (0, 0)),
        ],
        out_specs=[pl.BlockSpec((tm, cols), lambda i: (i, 0))] * 2,
        compiler_params=_params("parallel"),
        name="ssd_dt",
    )(dt_raw, dt_bias.reshape(1, cols), a_log.reshape(1, cols))


def _ssd_kernel(nc, has_h0, has_ht, n_prev, *refs):
    (z_ref, x_ref, b_ref, c_ref, acsc_ref, ldtr_ref, acsr_ref, dsk_ref, nw_ref,
     cwx_ref, cwb_ref, cwc_ref, cbx_ref, cbb_ref, cbc_ref) = refs[:15]
    pos = 15
    h0_ref = None
    ht_ref = None
    prev_ref = None
    if has_h0:
        h0_ref = refs[pos]
        pos += 1
    if n_prev:
        prev_ref = refs[pos]
        pos += 1
    y_ref = refs[pos]
    pos += 1
    if has_ht:
        ht_ref = refs[pos]
        pos += 1
    xb_scr, xt_scr, bs_scr, cs_scr, yf_scr, yb_scr, h_scr = refs[pos:]

    seq_len = nc * CHUNK
    pairs = x_ref.shape[1] // LANES
    row = lax.broadcasted_iota(jnp.int32, (CHUNK, 1), 0)
    upper = row < SSD_HEAD_DIM
    lo = lax.broadcasted_iota(jnp.int32, (1, LANES), 1) < SSD_HEAD_DIM
    ti = lax.broadcasted_iota(jnp.int32, (CHUNK, CHUNK), 0)
    si = lax.broadcasted_iota(jnp.int32, (CHUNK, CHUNK), 1)

    def conv_silu(src, w_ref, bias_ref, c, r0):
        rp = pl.multiple_of(jnp.maximum(r0 - SUBLANES, 0), SUBLANES)
        rn = pl.multiple_of(jnp.minimum(r0 + CHUNK, seq_len - SUBLANES), SUBLANES)
        cur = src[pl.ds(r0, CHUNK), :]
        prow = jnp.where(c > 0, src[pl.ds(rp, SUBLANES), :][SUBLANES - 1:SUBLANES, :], 0.0)
        nrow = jnp.where(c < nc - 1, src[pl.ds(rn, SUBLANES), :][0:1, :], 0.0)
        xm1 = jnp.where(row == 0, prow, pltpu.roll(cur, 1, 0))
        xp1 = jnp.where(row == CHUNK - 1, nrow, pltpu.roll(cur, CHUNK - 1, 0))
        w = w_ref[...]
        return _silu(w[0:1, :] * xm1 + w[1:2, :] * cur + w[2:3, :] * xp1 + bias_ref[...])

    def conv_chunk(c, carry):
        r0 = pl.multiple_of(c * CHUNK, CHUNK)
        xs = conv_silu(x_ref, cwx_ref, cbx_ref, c, r0)
        xb_scr[pl.ds(r0, CHUNK), :] = xs.astype(BF16)
        yf_scr[pl.ds(r0, CHUNK), :] = dsk_ref[...] * xs
        for p in range(pairs):
            cols = slice(LANES * p, LANES * (p + 1))
            xt_scr[cols, pl.ds(r0, CHUNK)] = xs[:, cols].T
        bs_scr[pl.ds(r0, CHUNK), :] = conv_silu(b_ref, cwb_ref, cbb_ref, c, r0).astype(BF16)
        cs_scr[pl.ds(r0, CHUNK), :] = conv_silu(c_ref, cwc_ref, cbc_ref, c, r0).astype(BF16)
        return carry

    def chunk_step(c, d):
        r0 = pl.multiple_of(c * CHUNK, CHUNK)
        xb = xb_scr[pl.ds(r0, CHUNK), :]
        bc = bs_scr[pl.ds(r0, CHUNK), :]
        cc = cs_scr[pl.ds(r0, CHUNK), :]
        acs_c = acsc_ref[pl.ds(r0, CHUNK), :]
        acs_r = acsr_ref[:, pl.ds(r0, CHUNK)]
        ldt_r = ldtr_ref[:, pl.ds(r0, CHUNK)]
        last = CHUNK - 1 if d == 0 else 0
        end_r = jnp.broadcast_to(acs_r[:, last:last + 1], acs_r.shape)
        w_r = jnp.exp2(end_r + ldt_r)
        tot_r = jnp.exp2(end_r)
        cb = lax.dot_general(cc, bc, NT_DIMS, preferred_element_type=F32)
        mask = (ti >= si) if d == 0 else (ti <= si)

        def head(k):
            col = jnp.broadcast_to(acs_c[:, k:k + 1], (CHUNK, CHUNK))
            decay = jnp.exp2(jnp.where(mask, col + ldt_r[k:k + 1, :], -jnp.inf))
            return col, (cb * decay).astype(BF16)

        for p in range(pairs):
            k0 = SSD_HEADS_PER_GROUP * d + 2 * p
            k1 = k0 + 1
            cols = slice(LANES * p, LANES * (p + 1))
            col0, m0 = head(k0)
            col1, m1 = head(k1)
            xbp = xb[:, cols]
            zero = jnp.zeros_like(xbp)
            y = jnp.dot(m0, jnp.where(lo, xbp, zero), preferred_element_type=F32)
            y += jnp.dot(m1, jnp.where(lo, zero, xbp), preferred_element_type=F32)
            hp = h_scr[d, cols, :]
            y_inter = lax.dot_general(cc, hp.astype(BF16), NT_DIMS, preferred_element_type=F32)
            y += y_inter * jnp.exp2(jnp.where(lo, col0, col1))
            xw = xt_scr[cols, pl.ds(r0, CHUNK)] * jnp.where(upper, w_r[k0:k0 + 1, :], w_r[k1:k1 + 1, :])
            s_new = jnp.dot(xw.astype(BF16), bc, preferred_element_type=F32)
            h_scr[d, cols, :] = jnp.where(upper, tot_r[k0:k0 + 1, :], tot_r[k1:k1 + 1, :]) * hp + s_new
            if d == 0:
                yf_scr[pl.ds(r0, CHUNK), cols] += y
            else:
                yb_scr[pl.ds(r0, CHUNK), cols] = y

    if has_h0:
        h_scr[...] = h0_ref[...]
    else:
        h_scr[...] = jnp.zeros_like(h_scr)
    lax.fori_loop(0, nc, conv_chunk, 0)

    def scan_body(i, carry):
        chunk_step(i, 0)
        chunk_step(nc - 1 - i, 1)
        return carry

    lax.fori_loop(0, nc, scan_body, 0, unroll=min(nc, 4))

    def finish(c, carry):
        r0 = pl.multiple_of(c * CHUNK, CHUNK)
        y = yf_scr[pl.ds(r0, CHUNK), :] + yb_scr[pl.ds(r0, CHUNK), :]
        g = y * _silu(z_ref[pl.ds(r0, CHUNK), :])
        y_ref[pl.ds(r0, CHUNK), :] = _rms(g, nw_ref[...]).astype(y_ref.dtype)
        return carry

    lax.fori_loop(0, nc, finish, 0)
    if has_ht:
        if n_prev:
            ht_ref[0:n_prev] = prev_ref[...]
        ht_ref[n_prev] = h_scr[...]


def _ssd_scan(proj, acs_c, ldt_r, acs_r, dsk, norm_w, conv_w, conv_b, row0, nseq, seq_len, h0, want_state,
              prev_states=None):
    n_prev = 0 if prev_states is None else prev_states.shape[1]
    groups = acs_c.shape[0]
    gw = SSD_HEADS_PER_GROUP * SSD_HEAD_DIM
    d_inner = groups * gw
    n = SSD_D_STATE
    nc = seq_len // CHUNK
    rb0 = row0 // seq_len
    xoff = d_inner // gw
    boff = 2 * d_inner // n
    coff = boff + groups
    cwx_off = 0
    cwb_off = d_inner // n
    cwc_off = cwb_off + groups

    in_specs = [
        pl.BlockSpec((seq_len, gw), lambda b, g: (rb0 + b, g)),
        pl.BlockSpec((seq_len, gw), lambda b, g: (rb0 + b, xoff + g)),
        pl.BlockSpec((seq_len, n), lambda b, g: (rb0 + b, boff + g)),
        pl.BlockSpec((seq_len, n), lambda b, g: (rb0 + b, coff + g)),
        pl.BlockSpec((None, seq_len, 2 * SSD_HEADS_PER_GROUP), lambda b, g: (g, rb0 + b, 0)),
        pl.BlockSpec((None, 2 * SSD_HEADS_PER_GROUP, seq_len), lambda b, g: (g, 0, rb0 + b)),
        pl.BlockSpec((None, 2 * SSD_HEADS_PER_GROUP, seq_len), lambda b, g: (g, 0, rb0 + b)),
        pl.BlockSpec((1, gw), lambda b, g: (0, g)),
        pl.BlockSpec((1, gw), lambda b, g: (0, g)),
        pl.BlockSpec((SSD_CONV_W, gw), lambda b, g: (0, cwx_off + g)),
        pl.BlockSpec((SSD_CONV_W, n), lambda b, g: (0, cwb_off + g)),
        pl.BlockSpec((SSD_CONV_W, n), lambda b, g: (0, cwc_off + g)),
        pl.BlockSpec((1, gw), lambda b, g: (0, cwx_off + g)),
        pl.BlockSpec((1, n), lambda b, g: (0, cwb_off + g)),
        pl.BlockSpec((1, n), lambda b, g: (0, cwc_off + g)),
    ]
    args = [proj, proj, proj, proj, acs_c, ldt_r, acs_r, dsk, norm_w,
            conv_w, conv_w, conv_w, conv_b, conv_b, conv_b]
    if h0 is not None:
        in_specs.append(pl.BlockSpec((None, 2, gw, n), lambda b, g: (b, 0, g, 0)))
        args.append(h0)
    if n_prev:
        in_specs.append(pl.BlockSpec((None, n_prev, 2, gw, n), lambda b, g: (b, 0, 0, g, 0)))
        args.append(prev_states)
    out_shape = [jax.ShapeDtypeStruct((nseq * seq_len, d_inner), BF16)]
    out_specs = [pl.BlockSpec((seq_len, gw), lambda b, g: (b, g))]
    if want_state:
        out_shape.append(jax.ShapeDtypeStruct((nseq, n_prev + 1, 2, d_inner, n), F32))
        out_specs.append(pl.BlockSpec((None, n_prev + 1, 2, gw, n), lambda b, g: (b, 0, 0, g, 0)))
    outs = pl.pallas_call(
        functools.partial(_ssd_kernel, nc, h0 is not None, want_state, n_prev),
        out_shape=out_shape,
        grid=(nseq, groups),
        in_specs=in_specs,
        out_specs=out_specs,
        scratch_shapes=[
            pltpu.VMEM((seq_len, gw), BF16),
            pltpu.VMEM((gw, seq_len), F32),
            pltpu.VMEM((seq_len, n), BF16),
            pltpu.VMEM((seq_len, n), BF16),
            pltpu.VMEM((seq_len, gw), F32),
            pltpu.VMEM((seq_len, gw), F32),
            pltpu.VMEM((2, gw, n), F32),
        ],
        compiler_params=_params("parallel", "parallel"),
        name="ssd_scan",
    )(*args)
    return outs if want_state else (outs[0], None)


def _ssd_mixer(x, norm_w_mix, mod5, layer, tok, geo, h0_sample, prev_states, j, w_in, conv_w, conv_b, dt_bias,
               a_log, d_skip, norm_w, w_out):
    heads = dt_bias.shape[1]
    groups = heads // SSD_HEADS_PER_GROUP
    d_inner = heads * SSD_HEAD_DIM
    hpg = SSD_HEADS_PER_GROUP
    t = x.shape[0]
    proj = _norm_mod_matmul(x, norm_w_mix, mod5, layer, tok, w_in, j)
    dt_off = d_inner + conv_w.shape[1]
    ldt, acs = _ssd_dt(proj[:, dt_off:dt_off + 2 * heads], dt_bias, a_log)

    def per_group(v):
        return v.reshape(t, 2, groups, hpg).transpose(2, 0, 1, 3).reshape(groups, t, 2 * hpg)

    acs_c = per_group(acs)
    acs_r = acs_c.transpose(0, 2, 1)
    ldt_r = per_group(ldt).transpose(0, 2, 1)
    dsk = jnp.repeat(d_skip, SSD_HEAD_DIM).reshape(1, d_inner)
    nw = norm_w.reshape(1, d_inner)
    cb = conv_b.reshape(1, -1)
    bp, lp, bs, ls = geo
    y_p, states = _ssd_scan(proj, acs_c, ldt_r, acs_r, dsk, nw, conv_w, cb, 0, bp, lp, None, True, prev_states)
    h0 = h0_sample.reshape(bs, 2, d_inner, SSD_D_STATE)
    y_s, _ = _ssd_scan(proj, acs_c, ldt_r, acs_r, dsk, nw, conv_w, cb, bp * lp, bs, ls, h0, False)
    x = _matmul_residual(y_p, y_s, w_out, j, x, mod5, layer, tok)
    return x, states


def _rope(x, cos, sin):
    quarter = ATTN_HEAD_DIM // 4
    lane = lax.broadcasted_iota(jnp.int32, (1, ATTN_HEAD_DIM), 1)
    first = (lane % (2 * quarter)) < quarter
    swapped = jnp.where(first, pltpu.roll(x, ATTN_HEAD_DIM - quarter, 1), pltpu.roll(x, quarter, 1))
    return x * cos + swapped * sin


def _attn_kernel(nq, has_cache, use_rope, write_k, *refs):
    q_ref, k_ref, v_ref, qn_ref, kn_ref = refs[:5]
    pos = 5
    cos_ref = sin_ref = kc_ref = vc_ref = knew_ref = None
    if use_rope:
        cos_ref, sin_ref = refs[pos:pos + 2]
        pos += 2
    if has_cache:
        kc_ref, vc_ref = refs[pos:pos + 2]
        pos += 2
    o_ref = refs[pos]
    pos += 1
    if write_k:
        knew_ref = refs[pos]
        pos += 1
    kb_scr, vt_scr = refs[pos:pos + 2]
    pos += 2
    if has_cache:
        kcb_scr, vct_scr = refs[pos:pos + 2]
        kcb_scr[...] = kc_ref[...].astype(BF16)
        for c in range(vc_ref.shape[0] // CHUNK):
            rows = slice(c * CHUNK, (c + 1) * CHUNK)
            vct_scr[:, rows] = vc_ref[rows, :].T.astype(BF16)

    grp = q_ref.shape[1] // ATTN_HEAD_DIM
    scale = ATTN_HEAD_DIM ** -0.5

    def kprep(c, carry):
        r0 = pl.multiple_of(c * CHUNK, CHUNK)
        kn = _rms(k_ref[pl.ds(r0, CHUNK), :], kn_ref[...])
        if write_k:
            knew_ref[pl.ds(r0, CHUNK), :] = kn
        if use_rope:
            kn = _rope(kn, cos_ref[pl.ds(r0, CHUNK), :], sin_ref[pl.ds(r0, CHUNK), :])
        kb_scr[pl.ds(r0, CHUNK), :] = kn.astype(BF16)
        vt_scr[:, pl.ds(r0, CHUNK)] = v_ref[pl.ds(r0, CHUNK), :].T.astype(BF16)
        return carry

    lax.fori_loop(0, nq, kprep, 0)

    def qblock(c, carry):
        r0 = pl.multiple_of(c * CHUNK, CHUNK)
        q_t = []
        for g in range(grp):
            cols = slice(ATTN_HEAD_DIM * g, ATTN_HEAD_DIM * (g + 1))
            qn = _rms(q_ref[pl.ds(r0, CHUNK), cols], qn_ref[...])
            if use_rope:
                qn = _rope(qn, cos_ref[pl.ds(r0, CHUNK), :], sin_ref[pl.ds(r0, CHUNK), :])
            q_t.append((qn * (scale * LOG2_E)).T.astype(BF16))
        q_t = jnp.concatenate(q_t, axis=1)
        s = jnp.dot(kb_scr[...], q_t, preferred_element_type=F32)
        m = jnp.max(s, axis=0, keepdims=True)
        if has_cache:
            sc = jnp.dot(kcb_scr[...], q_t, preferred_element_type=F32)
            m = jnp.maximum(m, jnp.max(sc, axis=0, keepdims=True))
            ec = jnp.exp2(sc - m)
        e = jnp.exp2(s - m)
        den = jnp.sum(e, axis=0, keepdims=True)
        if has_cache:
            den = den + jnp.sum(ec, axis=0, keepdims=True)
        o_t = jnp.dot(vt_scr[...], e.astype(BF16), preferred_element_type=F32)
        if has_cache:
            o_t += jnp.dot(vct_scr[...], ec.astype(BF16), preferred_element_type=F32)
        o_t = o_t * (1.0 / den)
        for g in range(grp):
            cols = slice(ATTN_HEAD_DIM * g, ATTN_HEAD_DIM * (g + 1))
            o_ref[pl.ds(r0, CHUNK), cols] = o_t[:, cols].T.astype(o_ref.dtype)
        return carry

    lax.fori_loop(0, nq, qblock, 0)


def _attention(qkv, q_norm, k_norm, heads, kv_heads, row0, nseq, seq_len, rope, cache, write_k):
    dh = ATTN_HEAD_DIM
    grp = heads // kv_heads
    rb0 = row0 // seq_len
    nq = seq_len // CHUNK
    in_specs = [
        pl.BlockSpec((seq_len, grp * dh), lambda b, h: (rb0 + b, h)),
        pl.BlockSpec((seq_len, dh), lambda b, h: (rb0 + b, heads + h)),
        pl.BlockSpec((seq_len, dh), lambda b, h: (rb0 + b, heads + kv_heads + h)),
        pl.BlockSpec((1, dh), lambda b, h: (0, 0)),
        pl.BlockSpec((1, dh), lambda b, h: (0, 0)),
    ]
    args = [qkv, qkv, qkv, q_norm.reshape(1, dh), k_norm.reshape(1, dh)]
    scratch = [pltpu.VMEM((seq_len, dh), BF16), pltpu.VMEM((dh, seq_len), BF16)]
    if rope is not None:
        in_specs += [pl.BlockSpec((seq_len, dh), lambda b, h: (0, 0))] * 2
        args += list(rope)
    if cache is not None:
        past = cache[0].shape[1]
        in_specs += [pl.BlockSpec((None, past, dh), lambda b, h: (b, 0, h))] * 2
        args += list(cache)
        scratch += [pltpu.VMEM((past, dh), BF16), pltpu.VMEM((dh, past), BF16)]
    out_shape = [jax.ShapeDtypeStruct((nseq * seq_len, heads * dh), BF16)]
    out_specs = [pl.BlockSpec((seq_len, grp * dh), lambda b, h: (b, h))]
    if write_k:
        out_shape.append(jax.ShapeDtypeStruct((nseq * seq_len, kv_heads * dh), F32))
        out_specs.append(pl.BlockSpec((seq_len, dh), lambda b, h: (b, h)))
    outs = pl.pallas_call(
        functools.partial(_attn_kernel, nq, cache is not None, rope is not None, write_k),
        out_shape=out_shape,
        grid=(nseq, kv_heads),
        in_specs=in_specs,
        out_specs=out_specs,
        scratch_shapes=scratch,
        compiler_params=_params("parallel", "parallel"),
        name="attention",
    )(*args)
    return outs if write_k else (outs[0], None)


def _rope_tables(seq_len):
    quarter = ATTN_HEAD_DIM // 4
    t = jnp.arange(seq_len)
    lane = jnp.arange(ATTN_HEAD_DIM)
    pos = jnp.where(lane[None, :] < 2 * quarter, (t // GRID_W)[:, None], (t % GRID_W)[:, None]).astype(F32)
    inv_freq = ROPE_THETA ** (-jnp.arange(quarter, dtype=F32) / quarter)
    ang = pos * inv_freq[lane % quarter][None, :]
    sign = jnp.where((lane % (2 * quarter)) < quarter, -1.0, 1.0).astype(F32)
    return jnp.cos(ang), jnp.sin(ang) * sign[None, :]


def _attn_mixer(x, norm_w_mix, mod5, layer, tok, geo, cache_k, cache_v, j, w_qkv, q_norm, k_norm, w_out):
    bp, lp, bs, ls = geo
    dh = ATTN_HEAD_DIM
    kv_heads = cache_k.shape[2]
    heads = w_out.shape[1] // dh
    qkv = _norm_mod_matmul(x, norm_w_mix, mod5, layer, tok, w_qkv, j)
    o_p, k_new = _attention(qkv, q_norm, k_norm, heads, kv_heads, 0, bp, lp, None, None, True)
    past = cache_k.shape[1]
    cache = (cache_k.reshape(bs, past, kv_heads * dh), cache_v.reshape(bs, past, kv_heads * dh))
    o_s, _ = _attention(qkv, q_norm, k_norm, heads, kv_heads, bp * lp, bs, ls, _rope_tables(ls), cache, False)
    x = _matmul_residual(o_p, o_s, w_out, j, x, mod5, layer, tok)
    v_new = qkv[:bp * lp, (heads + kv_heads) * dh:]
    return x, k_new.reshape(bp, lp, kv_heads, dh), v_new.reshape(bp, lp, kv_heads, dh)


def _mlstm_gate_kernel(heads, i_ref, f_ref, bi_ref, bf_ref, b_ref, r_ref, p_ref):
    forward = lax.broadcasted_iota(jnp.int32, (1, i_ref.shape[1]), 1) < heads
    for c in range(i_ref.shape[0] // CHUNK):
        rows = slice(c * CHUNK, (c + 1) * CHUNK)
        log_f = _log_sigmoid(f_ref[rows, :] + bf_ref[...])
        b = jnp.where(forward, _cumsum_chunk(log_f, False), _cumsum_chunk(log_f, True))
        r = i_ref[rows, :] + bi_ref[...] - b
        b_ref[rows, :] = b
        r_ref[rows, :] = r
        p_ref[rows, :] = jnp.where(forward, _cummax_chunk(r, False), _cummax_chunk(r, True))


def _mlstm_gates(i_raw, f_raw, b_i, b_f):
    t, cols = i_raw.shape
    tm = _pick_tile(t, (1024, 512, 256, 128))
    return pl.pallas_call(
        functools.partial(_mlstm_gate_kernel, cols // 2),
        out_shape=[jax.ShapeDtypeStruct((t, cols), F32)] * 3,
        grid=(t // tm,),
        in_specs=[
            pl.BlockSpec((tm, cols), lambda i: (i, 0)),
            pl.BlockSpec((tm, cols), lambda i: (i, 0)),
            pl.BlockSpec((1, cols), lambda i: (0, 0)),
            pl.BlockSpec((1, cols), lambda i: (0, 0)),
        ],
        out_specs=[pl.BlockSpec((tm, cols), lambda i: (i, 0))] * 3,
        compiler_params=_params("parallel"),
        name="mlstm_gates",
    )(i_raw, f_raw, b_i, b_f)


def _mlstm_kernel(nc, has_s0, has_st, *refs):
    q_ref, k_ref, v_ref, og_ref, cp_ref, rr_ref, hn_ref = refs[:7]
    pos = 7
    c0_ref = n0_ref = m0_ref = ct_ref = nt_ref = mt_ref = None
    if has_s0:
        c0_ref, n0_ref, m0_ref = refs[pos:pos + 3]
        pos += 3
    h_ref = refs[pos]
    pos += 1
    if has_st:
        ct_ref, nt_ref, mt_ref = refs[pos:pos + 3]
        pos += 3
    hf_scr, hb_scr, c_scr, n_scr, m_scr = refs[pos:]

    dk = q_ref.shape[1]
    dv = v_ref.shape[1]
    qscale = dk ** -0.5
    tile = (CHUNK, CHUNK)
    ti = lax.broadcasted_iota(jnp.int32, tile, 0)
    si = lax.broadcasted_iota(jnp.int32, tile, 1)

    def wide(x):
        return jnp.concatenate([x] * (dv // CHUNK), axis=1)

    def chunk_step(c, d):
        r0 = pl.multiple_of(c * CHUNK, CHUNK)
        q = q_ref[pl.ds(r0, CHUNK), :] * qscale
        k = k_ref[pl.ds(r0, CHUNK), :]
        qb = q.astype(BF16)
        kb = k.astype(BF16)
        vb = v_ref[pl.ds(r0, CHUNK), :].astype(BF16)
        cp = cp_ref[pl.ds(r0, CHUNK), :]
        b_t = jnp.broadcast_to(cp[:, 3 * d:3 * d + 1], tile)
        r_t = jnp.broadcast_to(cp[:, 3 * d + 1:3 * d + 2], tile)
        p_t = jnp.broadcast_to(cp[:, 3 * d + 2:3 * d + 3], tile)
        r_row = rr_ref[d:d + 1, pl.ds(r0, CHUNK)]
        m_row = m_scr[d]
        u_t = jnp.maximum(m_row, p_t)
        mask = (ti >= si) if d == 0 else (ti <= si)
        s = lax.dot_general(qb, kb, NT_DIMS, preferred_element_type=F32)
        s = s * jnp.exp(jnp.where(mask, r_row - u_t, -jnp.inf))
        inter = jnp.exp(m_row - u_t)
        cst = c_scr[d]
        nst = n_scr[d]
        num = jnp.dot(s.astype(BF16), vb, preferred_element_type=F32)
        num += wide(inter) * jnp.dot(qb, cst.astype(BF16), preferred_element_type=F32)
        den = jnp.sum(s + inter * (q * nst), axis=1, keepdims=True)
        rcp = 1.0 / jnp.maximum(jnp.abs(den), jnp.exp(-(b_t + u_t)))
        h = num * wide(rcp)
        last = CHUNK - 1 if d == 0 else 0
        u_last = u_t[last:last + 1, :]
        wk = jnp.exp(r_t - u_last) * k
        carry_decay = jnp.exp(m_row - u_last)
        c_scr[d] = wide(carry_decay) * cst + jnp.dot(wk.T.astype(BF16), vb, preferred_element_type=F32)
        n_scr[d] = carry_decay * nst + jnp.sum(wk, axis=0, keepdims=True)
        m_scr[d] = b_t[last:last + 1, :] + u_last
        (hf_scr if d == 0 else hb_scr)[pl.ds(r0, CHUNK), :] = h

    for d in range(2):
        if has_s0:
            c_scr[d] = c0_ref[d]
            n_scr[d] = n0_ref[d]
            m_scr[d] = m0_ref[d]
        else:
            c_scr[d] = jnp.zeros(c_scr.shape[1:], F32)
            n_scr[d] = jnp.zeros(n_scr.shape[1:], F32)
            m_scr[d] = jnp.zeros(m_scr.shape[1:], F32)

    def scan_body(i, carry):
        chunk_step(i, 0)
        chunk_step(nc - 1 - i, 1)
        return carry

    lax.fori_loop(0, nc, scan_body, 0, unroll=2)

    def finish(c, carry):
        r0 = pl.multiple_of(c * CHUNK, CHUNK)
        hs = _rms(hf_scr[pl.ds(r0, CHUNK), :] + hb_scr[pl.ds(r0, CHUNK), :], hn_ref[...])
        h_ref[pl.ds(r0, CHUNK), :] = (hs * _sigmoid(og_ref[pl.ds(r0, CHUNK), :])).astype(h_ref.dtype)
        return carry

    lax.fori_loop(0, nc, finish, 0)
    if has_st:
        for d in range(2):
            ct_ref[d] = c_scr[d]
            nt_ref[d] = n_scr[d]
            mt_ref[d] = m_scr[d]


def _mlstm_scan(proj, gate_cols, gate_rows, head_norm, heads, dk, dv, row0, nseq, seq_len, state, want_state):
    assert dv == 2 * dk and dk == CHUNK
    nc = seq_len // CHUNK
    rb0 = row0 // seq_len
    in_specs = [
        pl.BlockSpec((seq_len, dk), lambda b, h: (rb0 + b, h)),
        pl.BlockSpec((seq_len, dk), lambda b, h: (rb0 + b, heads + h)),
        pl.BlockSpec((seq_len, dv), lambda b, h: (rb0 + b, heads + h)),
        pl.BlockSpec((seq_len, dv), lambda b, h: (rb0 + b, 2 * heads + h)),
        pl.BlockSpec((None, seq_len, 6), lambda b, h: (h, rb0 + b, 0)),
        pl.BlockSpec((None, 2, seq_len), lambda b, h: (h, 0, rb0 + b)),
        pl.BlockSpec((None, 1, dv), lambda b, h: (h, 0, 0)),
    ]
    args = [proj, proj, proj, proj, gate_cols, gate_rows, head_norm]
    state_specs = [
        pl.BlockSpec((None, 2, None, dk, dv), lambda b, h: (b, 0, h, 0, 0)),
        pl.BlockSpec((None, 2, None, 1, dk), lambda b, h: (b, 0, h, 0, 0)),
        pl.BlockSpec((None, 2, None, 1, LANES), lambda b, h: (b, 0, h, 0, 0)),
    ]
    if state is not None:
        in_specs += state_specs
        args += list(state)
    out_shape = [jax.ShapeDtypeStruct((nseq * seq_len, heads * dv), BF16)]
    out_specs = [pl.BlockSpec((seq_len, dv), lambda b, h: (b, h))]
    if want_state:
        out_shape += [jax.ShapeDtypeStruct((nseq, 2, heads, dk, dv), F32),
                      jax.ShapeDtypeStruct((nseq, 2, heads, 1, dk), F32),
                      jax.ShapeDtypeStruct((nseq, 2, heads, 1, LANES), F32)]
        out_specs += state_specs
    outs = pl.pallas_call(
        functools.partial(_mlstm_kernel, nc, state is not None, want_state),
        out_shape=out_shape,
        grid=(nseq, heads),
        in_specs=in_specs,
        out_specs=out_specs,
        scratch_shapes=[
            pltpu.VMEM((seq_len, dv), F32),
            pltpu.VMEM((seq_len, dv), F32),
            pltpu.VMEM((2, dk, dv), F32),
            pltpu.VMEM((2, 1, dk), F32),
            pltpu.VMEM((2, 1, LANES), F32),
        ],
        compiler_params=_params("parallel", "parallel"),
        name="mlstm_scan",
    )(*args)
    return outs


def _mlstm_mixer(x, norm_w_mix, mod5, layer, tok, geo, c0, n0, m0, j, w_in, b_gates, head_norm, w_out):
    bp, lp, bs, ls = geo
    heads, dk, dv = c0.shape[2], c0.shape[3], c0.shape[4]
    t = x.shape[0]
    proj = _norm_mod_matmul(x, norm_w_mix, mod5, layer, tok, w_in, j)
    g_off = 2 * heads * dk + 2 * heads * dv
    graw = proj[:, g_off:g_off + 4 * heads].reshape(t, 2, 2, heads)
    b, r, p = _mlstm_gates(graw[:, :, 0, :].reshape(t, 2 * heads), graw[:, :, 1, :].reshape(t, 2 * heads),
                           b_gates[:, 0, :].reshape(1, 2 * heads), b_gates[:, 1, :].reshape(1, 2 * heads))
    gate_cols = jnp.stack([b, r, p], axis=-1).reshape(t, 2, heads, 3).transpose(2, 0, 1, 3).reshape(heads, t, 6)
    gate_rows = r.reshape(t, 2, heads).transpose(2, 1, 0)
    hn = head_norm.reshape(heads, 1, dv)
    outs_p = _mlstm_scan(proj, gate_cols, gate_rows, hn, heads, dk, dv, 0, bp, lp, None, True)
    h_p, c_t, n_t, m_t = outs_p
    state = (c0, n0.reshape(bs, 2, heads, 1, dk),
             jnp.broadcast_to(m0[..., None, None], (bs, 2, heads, 1, LANES)))
    (h_s,) = _mlstm_scan(proj, gate_cols, gate_rows, hn, heads, dk, dv, bp * lp, bs, ls, state, False)
    x = _matmul_residual(h_p, h_s, w_out, j, x, mod5, layer, tok)
    return x, c_t, n_t.reshape(bp, 2, heads, dk), m_t[:, :, :, 0, 0]


def kernel(x_prompt, x_sample, c, state_ssd, cache_attn_k, cache_attn_v, state_mlstm_C, state_mlstm_n, state_mlstm_m, c_ctx, ada_w, ada_b, norm_mix_w, norm_ffn_w, ffn_w_gate, ffn_w_up, ffn_w_down, ssd_w_in, ssd_conv_w, ssd_conv_b, ssd_dt_bias, ssd_a_log, ssd_d, ssd_norm_w, ssd_w_out, attn_w_qkv, attn_q_norm, attn_k_norm, attn_w_out, mlstm_w_in, mlstm_b_gates, mlstm_head_norm, mlstm_w_out):
    bp, lp, d = x_prompt.shape
    bs, ls, _ = x_sample.shape
    depth = ada_w.shape[0]
    geo = (bp, lp, bs, ls)
    tok = _Tokens(bp * lp, ls)
    x = jnp.concatenate([x_prompt.reshape(bp * lp, d), x_sample.reshape(bs * ls, d)], axis=0)

    mod_rows = -(-(1 + bs) // SUBLANES) * SUBLANES
    cc = jnp.concatenate([c_ctx[None, :], c, jnp.zeros((mod_rows - 1 - bs, d), F32)], axis=0)
    mod5 = _modulation(cc, ada_w, ada_b).reshape(depth, mod_rows, 6, 1, d)

    w_gate, w_up, w_down = _to_bf16(ffn_w_gate), _to_bf16(ffn_w_up), _to_bf16(ffn_w_down)
    ssd_in, ssd_out = _to_bf16(ssd_w_in, 6 * MXU_COLS), _to_bf16(ssd_w_out)
    attn_qkv, attn_out = _to_bf16(attn_w_qkv), _to_bf16(attn_w_out)
    mlstm_in = _to_bf16(mlstm_w_in, 5 * MXU_COLS)
    mlstm_out = _to_bf16(mlstm_w_out)
    ssd_states = None
    new_k, new_v, new_c, new_n, new_m = [], [], [], [], []
    for l in range(depth):
        kind, j = l % 3, l // 3
        if kind == 0:
            x, ssd_states = _ssd_mixer(x, norm_mix_w[l], mod5, l, tok, geo, state_ssd[:, j], ssd_states, j, ssd_in,
                                       ssd_conv_w[j], ssd_conv_b[j], ssd_dt_bias[j], ssd_a_log[j], ssd_d[j],
                                       ssd_norm_w[j], ssd_out)
        elif kind == 1:
            x, k_new, v_new = _attn_mixer(x, norm_mix_w[l], mod5, l, tok, geo, cache_attn_k[:, j], cache_attn_v[:, j],
                                          j, attn_qkv, attn_q_norm[j], attn_k_norm[j], attn_out)
            new_k.append(k_new)
            new_v.append(v_new)
        else:
            x, c_t, n_t, m_t = _mlstm_mixer(x, norm_mix_w[l], mod5, l, tok, geo, state_mlstm_C[:, j],
                                            state_mlstm_n[:, j], state_mlstm_m[:, j], j, mlstm_in,
                                            mlstm_b_gates[j], mlstm_head_norm[j], mlstm_out)
            new_c.append(c_t)
            new_n.append(n_t)
            new_m.append(m_t)
        x = _ffn(x, norm_ffn_w[l], mod5, l, tok, w_gate, w_up, w_down, split_out=(l == depth - 1))

    y_prompt = x[0].reshape(bp, lp, d)
    y_sample = x[1].reshape(bs, ls, d)
    ssd_states = ssd_states.reshape((bp,) + state_ssd.shape[1:])
    return (y_prompt, y_sample, ssd_states, jnp.stack(new_k, axis=1), jnp.stack(new_v, axis=1),
            jnp.stack(new_c, axis=1), jnp.stack(new_n, axis=1), jnp.stack(new_m, axis=1))
```

```python
import functools

import jax
import jax.numpy as jnp
from jax import lax
from jax.experimental import pallas as pl
from jax.experimental.pallas import tpu as pltpu

F32 = jnp.float32
BF16 = jnp.bfloat16

EPS = 1e-6
CHUNK = 128
GRID_W = 64
ROPE_THETA = 10000.0
SSD_HEAD_DIM = 64
SSD_D_STATE = 128
SSD_CONV_W = 3
SSD_HEADS_PER_GROUP = 8
ATTN_HEAD_DIM = 128
LANES = 128
SUBLANES = 8
MXU_COLS = 256
NORM_ROWS = 16
VMEM_LIMIT_BYTES = 56 * 1024 * 1024

NT_DIMS = (((1,), (1,)), ((), ()))
LOG2_E = 1.4426950408889634


def _sigmoid(x):
    return 0.5 + 0.5 * jnp.tanh(0.5 * x)


def _silu(x):
    h = 0.5 * x
    return h + h * jnp.tanh(h)


def _softplus(x):
    return jnp.maximum(x, 0.0) + jnp.log1p(jnp.exp(-jnp.abs(x)))


def _log_sigmoid(x):
    return -_softplus(-x)


def _rms(x, w):
    return x * lax.rsqrt(jnp.mean(x * x, axis=-1, keepdims=True) + EPS) * w


def _params(*sem):
    return pltpu.CompilerParams(dimension_semantics=sem, vmem_limit_bytes=VMEM_LIMIT_BYTES)


def _pick_tile(n, candidates):
    for c in candidates:
        if n % c == 0:
            return c
    return n


def _scan_chunk(a, reverse, combine, identity):
    idx = lax.broadcasted_iota(jnp.int32, (CHUNK, 1), 0)
    sh = 1
    while sh < CHUNK:
        if reverse:
            a = combine(a, jnp.where(idx < CHUNK - sh, pltpu.roll(a, CHUNK - sh, 0), identity))
        else:
            a = combine(a, jnp.where(idx >= sh, pltpu.roll(a, sh, 0), identity))
        sh *= 2
    return a


def _cumsum_chunk(a, reverse):
    return _scan_chunk(a, reverse, jnp.add, 0.0)


def _cummax_chunk(a, reverse):
    return _scan_chunk(a, reverse, jnp.maximum, -jnp.inf)


def _cast_kernel(w_ref, o_ref):
    n = w_ref.shape[1]
    o_ref[:, :n] = w_ref[...].astype(BF16)
    if o_ref.shape[1] > n:
        o_ref[:, n:] = jnp.zeros((o_ref.shape[0], o_ref.shape[1] - n), BF16)


def _to_bf16(w, col_multiple=1):
    shape = w.shape
    w2 = w.reshape(-1, shape[-1])
    rows, n = w2.shape
    n_out = -(-n // col_multiple) * col_multiple
    if n_out != n and n % LANES:
        return jnp.pad(_to_bf16(w), [(0, 0)] * (len(shape) - 1) + [(0, n_out - n)])
    tr = _pick_tile(rows, (256, 128, 64, 32, 16))
    out = pl.pallas_call(
        _cast_kernel,
        out_shape=jax.ShapeDtypeStruct((rows, n_out), BF16),
        grid=(rows // tr,),
        in_specs=[pl.BlockSpec((tr, n), lambda i: (i, 0))],
        out_specs=pl.BlockSpec((tr, n_out), lambda i: (i, 0)),
        compiler_params=_params("parallel"),
        name="cast_bf16",
    )(w2)
    return out.reshape(shape[:-1] + (n_out,))


def _mod_kernel(c_ref, w_ref, b_ref, o_ref):
    s = _silu(c_ref[...]).astype(BF16)
    o_ref[...] = jnp.dot(s, w_ref[...].astype(BF16), preferred_element_type=F32) + b_ref[...]


def _modulation(cc, ada_w, ada_b):
    depth, d, n = ada_w.shape
    rows = cc.shape[0]
    tn = _pick_tile(n, (1024, 512, 256, 128))
    return pl.pallas_call(
        _mod_kernel,
        out_shape=jax.ShapeDtypeStruct((depth, rows, n), F32),
        grid=(depth, n // tn),
        in_specs=[
            pl.BlockSpec((rows, d), lambda l, j: (0, 0)),
            pl.BlockSpec((None, d, tn), lambda l, j: (l, 0, j)),
            pl.BlockSpec((None, 1, tn), lambda l, j: (l, 0, j)),
        ],
        out_specs=pl.BlockSpec((None, rows, tn), lambda l, j: (l, 0, j)),
        compiler_params=_params("parallel", "parallel"),
        name="adaln_mod",
    )(cc, ada_w, ada_b.reshape(depth, 1, n))


class _Tokens:
    def __init__(self, n_prompt_tok, dec_seq):
        self.n_prompt_tok = n_prompt_tok
        self.dec_seq = dec_seq

    def row(self, start):
        return jnp.where(start < self.n_prompt_tok, 0, 1 + (start - self.n_prompt_tok) // self.dec_seq)

    def tile(self, candidates):
        for c in candidates:
            if self.n_prompt_tok % c == 0 and self.dec_seq % c == 0:
                return c
        raise ValueError("no token tile fits the sequence layout")


def _mod_spec(tok, tm, layer, k, d, grid_pos):
    def index_map(*ids):
        return (layer, tok.row(ids[grid_pos] * tm), k, 0, 0)
    return pl.BlockSpec((None, None, None, 1, d), index_map)


def _norm_mod(u_ref, x_ref, nw_ref, sh_ref, sc_ref):
    gain = nw_ref[...] * (1.0 + sc_ref[...])
    shift = sh_ref[...]

    def body(r, carry):
        r0 = pl.multiple_of(r * NORM_ROWS, NORM_ROWS)
        x = x_ref[pl.ds(r0, NORM_ROWS), :]
        inv = lax.rsqrt(jnp.mean(x * x, axis=-1, keepdims=True) + EPS)
        u_ref[pl.ds(r0, NORM_ROWS), :] = (x * inv * gain + shift).astype(BF16)
        return carry

    lax.fori_loop(0, x_ref.shape[0] // NORM_ROWS, body, 0, unroll=4)


def _nmm_kernel(x_ref, nw_ref, sh_ref, sc_ref, w_ref, o_ref, u_ref):
    @pl.when(pl.program_id(1) == 0)
    def _():
        _norm_mod(u_ref, x_ref, nw_ref, sh_ref, sc_ref)

    o_ref[...] = jnp.dot(u_ref[...], w_ref[...], preferred_element_type=F32)


def _norm_mod_matmul(x, norm_w, mod5, layer, tok, w, widx):
    t, d = x.shape
    n = w.shape[2]
    tm = tok.tile((1024, 512, 256, 128))
    tn = _pick_tile(n, (1536, 1280, 1024, 768, 512, 256, 128))
    return pl.pallas_call(
        _nmm_kernel,
        out_shape=jax.ShapeDtypeStruct((t, n), F32),
        grid=(t // tm, n // tn),
        in_specs=[
            pl.BlockSpec((tm, d), lambda i, j: (i, 0)),
            pl.BlockSpec((1, d), lambda i, j: (0, 0)),
            _mod_spec(tok, tm, layer, 0, d, 0),
            _mod_spec(tok, tm, layer, 1, d, 0),
            pl.BlockSpec((None, d, tn), lambda i, j: (widx, 0, j)),
        ],
        out_specs=pl.BlockSpec((tm, tn), lambda i, j: (i, j)),
        scratch_shapes=[pltpu.VMEM((tm, d), BF16)],
        compiler_params=_params("parallel", "arbitrary"),
        name="norm_mod_matmul",
    )(x, norm_w.reshape(1, d), mod5, mod5, w)


def _mmres_kernel(prompt_tiles, yp_ref, ys_ref, w_ref, x_ref, g_ref, o_ref):
    def run(y_ref):
        o_ref[...] = x_ref[...] + g_ref[...] * jnp.dot(y_ref[...], w_ref[...], preferred_element_type=F32)

    is_prompt = pl.program_id(1) < prompt_tiles
    pl.when(is_prompt)(lambda: run(yp_ref))
    pl.when(jnp.logical_not(is_prompt))(lambda: run(ys_ref))


def _matmul_residual(y_prompt, y_sample, w, widx, x, mod5, layer, tok):
    kdim = y_prompt.shape[1]
    t, d = x.shape
    tm = tok.tile((512, 256, 128))
    tn = _pick_tile(d, (1024, 512, 256, 128))
    nj = d // tn
    prompt_tiles = tok.n_prompt_tok // tm

    def gate_map(j, i):
        return (layer, tok.row(i * tm), 2, 0, j)

    return pl.pallas_call(
        functools.partial(_mmres_kernel, prompt_tiles),
        out_shape=jax.ShapeDtypeStruct((t, d), F32),
        grid=(nj, t // tm),
        in_specs=[
            pl.BlockSpec((tm, kdim), lambda j, i: (jnp.minimum(i, prompt_tiles - 1), 0)),
            pl.BlockSpec((tm, kdim), lambda j, i: (jnp.maximum(i - prompt_tiles, 0), 0)),
            pl.BlockSpec((None, kdim, tn), lambda j, i: (widx, 0, j)),
            pl.BlockSpec((tm, tn), lambda j, i: (i, j)),
            pl.BlockSpec((None, None, None, 1, tn), gate_map),
        ],
        out_specs=pl.BlockSpec((tm, tn), lambda j, i: (i, j)),
        compiler_params=_params("parallel", "parallel"),
        name="matmul_residual",
    )(y_prompt, y_sample, w, x, mod5)


def _ffn_kernel(prompt_tiles, x_ref, nw_ref, sh_ref, sc_ref, g_ref, wg_ref, wu_ref, wd_ref, *refs):
    u_ref, acc_ref = refs[-2:]
    j = pl.program_id(1)

    @pl.when(j == 0)
    def _():
        _norm_mod(u_ref, x_ref, nw_ref, sh_ref, sc_ref)
        acc_ref[...] = jnp.zeros_like(acc_ref)

    u = u_ref[...]
    a = jnp.dot(u, wg_ref[...], preferred_element_type=F32)
    b = jnp.dot(u, wu_ref[...], preferred_element_type=F32)
    h = (_silu(a) * b).astype(BF16)
    acc_ref[...] += jnp.dot(h, wd_ref[...], preferred_element_type=F32)

    last = j == pl.num_programs(1) - 1
    if prompt_tiles is None:
        (o_ref,) = refs[:-2]

        @pl.when(last)
        def _():
            o_ref[...] = x_ref[...] + g_ref[...] * acc_ref[...]
    else:
        op_ref, os_ref = refs[:-2]
        is_prompt = pl.program_id(0) < prompt_tiles

        @pl.when(jnp.logical_and(last, is_prompt))
        def _():
            op_ref[...] = x_ref[...] + g_ref[...] * acc_ref[...]

        @pl.when(jnp.logical_and(last, jnp.logical_not(is_prompt)))
        def _():
            os_ref[...] = x_ref[...] + g_ref[...] * acc_ref[...]


def _ffn(x, norm_w, mod5, layer, tok, wg, wu, wd, split_out=False):
    t, d = x.shape
    hid = wg.shape[2]
    tm = tok.tile((512, 256, 128))
    th = _pick_tile(hid, (512, 256, 128))
    if split_out:
        prompt_tiles = tok.n_prompt_tok // tm
        out_shape = [jax.ShapeDtypeStruct((tok.n_prompt_tok, d), F32),
                     jax.ShapeDtypeStruct((t - tok.n_prompt_tok, d), F32)]
        out_specs = [pl.BlockSpec((tm, d), lambda i, j: (jnp.minimum(i, prompt_tiles - 1), 0)),
                     pl.BlockSpec((tm, d), lambda i, j: (jnp.maximum(i - prompt_tiles, 0), 0))]
    else:
        prompt_tiles = None
        out_shape = jax.ShapeDtypeStruct((t, d), F32)
        out_specs = pl.BlockSpec((tm, d), lambda i, j: (i, 0))
    return pl.pallas_call(
        functools.partial(_ffn_kernel, prompt_tiles),
        out_shape=out_shape,
        grid=(t // tm, hid // th),
        in_specs=[
            pl.BlockSpec((tm, d), lambda i, j: (i, 0)),
            pl.BlockSpec((1, d), lambda i, j: (0, 0)),
            _mod_spec(tok, tm, layer, 3, d, 0),
            _mod_spec(tok, tm, layer, 4, d, 0),
            _mod_spec(tok, tm, layer, 5, d, 0),
            pl.BlockSpec((None, d, th), lambda i, j: (layer, 0, j)),
            pl.BlockSpec((None, d, th), lambda i, j: (layer, 0, j)),
            pl.BlockSpec((None, th, d), lambda i, j: (layer, j, 0)),
        ],
        out_specs=out_specs,
        scratch_shapes=[pltpu.VMEM((tm, d), BF16), pltpu.VMEM((tm, d), F32)],
        compiler_params=_params("arbitrary" if split_out else "parallel", "arbitrary"),
        name="ffn",
    )(x, norm_w.reshape(1, d), mod5, mod5, mod5, wg, wu, wd)


def _ssd_dt_kernel(heads, raw_ref, bias_ref, alog_ref, ldt_ref, acs_ref):
    forward = lax.broadcasted_iota(jnp.int32, (1, raw_ref.shape[1]), 1) < heads
    neg_a = -jnp.exp(alog_ref[...])
    for c in range(raw_ref.shape[0] // CHUNK):
        rows = slice(c * CHUNK, (c + 1) * CHUNK)
        dt = _softplus(raw_ref[rows, :] + bias_ref[...])
        a = dt * neg_a
        acs2 = jnp.where(forward, _cumsum_chunk(a, False), _cumsum_chunk(a, True)) * LOG2_E
        acs_ref[rows, :] = acs2
        ldt_ref[rows, :] = jnp.log(dt) * LOG2_E - acs2


def _ssd_dt(dt_raw, dt_bias, a_log):
    t, cols = dt_raw.shape
    tm = _pick_tile(t, (1024, 512, 256, 128))
    return pl.pallas_call(
        functools.partial(_ssd_dt_kernel, cols // 2),
        out_shape=[jax.ShapeDtypeStruct((t, cols), F32)] * 2,
        grid=(t // tm,),
        in_specs=[
            pl.BlockSpec((tm, cols), lambda i: (i, 0)),
            pl.BlockSpec((1, cols), lambda i: (0, 0)),
            pl.BlockSpec((1, cols), lambda i: (0, 0)),
        ],
        out_specs=[pl.BlockSpec((tm, cols), lambda i: (i, 0))] * 2,
        compiler_params=_params("parallel"),
        name="ssd_dt",
    )(dt_raw, dt_bias.reshape(1, cols), a_log.reshape(1, cols))


def _ssd_kernel(nc, has_h0, has_ht, n_prev, *refs):
    (z_ref, x_ref, b_ref, c_ref, acsc_ref, ldtr_ref, acsr_ref, dsk_ref, nw_ref,
     cwx_ref, cwb_ref, cwc_ref, cbx_ref, cbb_ref, cbc_ref) = refs[:15]
    pos = 15
    h0_ref = None
    ht_ref = None
    prev_ref = None
    if has_h0:
        h0_ref = refs[pos]
        pos += 1
    if n_prev:
        prev_ref = refs[pos]
        pos += 1
    y_ref = refs[pos]
    pos += 1
    if has_ht:
        ht_ref = refs[pos]
        pos += 1
    xb_scr, xt_scr, bs_scr, cs_scr, yf_scr, yb_scr, h_scr = refs[pos:]

    seq_len = nc * CHUNK
    pairs = x_ref.shape[1] // LANES
    row = lax.broadcasted_iota(jnp.int32, (CHUNK, 1), 0)
    upper = row < SSD_HEAD_DIM
    lo = lax.broadcasted_iota(jnp.int32, (1, LANES), 1) < SSD_HEAD_DIM
    ti = lax.broadcasted_iota(jnp.int32, (CHUNK, CHUNK), 0)
    si = lax.broadcasted_iota(jnp.int32, (CHUNK, CHUNK), 1)

    def conv_silu(src, w_ref, bias_ref, c, r0):
        rp = pl.multiple_of(jnp.maximum(r0 - SUBLANES, 0), SUBLANES)
        rn = pl.multiple_of(jnp.minimum(r0 + CHUNK, seq_len - SUBLANES), SUBLANES)
        cur = src[pl.ds(r0, CHUNK), :]
        prow = jnp.where(c > 0, src[pl.ds(rp, SUBLANES), :][SUBLANES - 1:SUBLANES, :], 0.0)
        nrow = jnp.where(c < nc - 1, src[pl.ds(rn, SUBLANES), :][0:1, :], 0.0)
        xm1 = jnp.where(row == 0, prow, pltpu.roll(cur, 1, 0))
        xp1 = jnp.where(row == CHUNK - 1, nrow, pltpu.roll(cur, CHUNK - 1, 0))
        w = w_ref[...]
        return _silu(w[0:1, :] * xm1 + w[1:2, :] * cur + w[2:3, :] * xp1 + bias_ref[...])

    def conv_chunk(c, carry):
        r0 = pl.multiple_of(c * CHUNK, CHUNK)
        xs = conv_silu(x_ref, cwx_ref, cbx_ref, c, r0)
        xb_scr[pl.ds(r0, CHUNK), :] = xs.astype(BF16)
        yf_scr[pl.ds(r0, CHUNK), :] = dsk_ref[...] * xs
        for p in range(pairs):
            cols = slice(LANES * p, LANES * (p + 1))
            xt_scr[cols, pl.ds(r0, CHUNK)] = xs[:, cols].T
        bs_scr[pl.ds(r0, CHUNK), :] = conv_silu(b_ref, cwb_ref, cbb_ref, c, r0).astype(BF16)
        cs_scr[pl.ds(r0, CHUNK), :] = conv_silu(c_ref, cwc_ref, cbc_ref, c, r0).astype(BF16)
        return carry

    def chunk_step(c, d):
        r0 = pl.multiple_of(c * CHUNK, CHUNK)
        xb = xb_scr[pl.ds(r0, CHUNK), :]
        bc = bs_scr[pl.ds(r0, CHUNK), :]
        cc = cs_scr[pl.ds(r0, CHUNK), :]
        acs_c = acsc_ref[pl.ds(r0, CHUNK), :]
        acs_r = acsr_ref[:, pl.ds(r0, CHUNK)]
        ldt_r = ldtr_ref[:, pl.ds(r0, CHUNK)]
        last = CHUNK - 1 if d == 0 else 0
        end_r = jnp.broadcast_to(acs_r[:, last:last + 1], acs_r.shape)
        w_r = jnp.exp2(end_r + ldt_r)
        tot_r = jnp.exp2(end_r)
        cb = lax.dot_general(cc, bc, NT_DIMS, preferred_element_type=F32)
        mask = (ti >= si) if d == 0 else (ti <= si)

        def head(k):
            col = jnp.broadcast_to(acs_c[:, k:k + 1], (CHUNK, CHUNK))
            decay = jnp.exp2(jnp.where(mask, col + ldt_r[k:k + 1, :], -jnp.inf))
            return col, (cb * decay).astype(BF16)

        for p in range(pairs):
            k0 = SSD_HEADS_PER_GROUP * d + 2 * p
            k1 = k0 + 1
            cols = slice(LANES * p, LANES * (p + 1))
            col0, m0 = head(k0)
            col1, m1 = head(k1)
            xbp = xb[:, cols]
            zero = jnp.zeros_like(xbp)
            y = jnp.dot(m0, jnp.where(lo, xbp, zero), preferred_element_type=F32)
            y += jnp.dot(m1, jnp.where(lo, zero, xbp), preferred_element_type=F32)
            hp = h_scr[d, cols, :]
            y_inter = lax.dot_general(cc, hp.astype(BF16), NT_DIMS, preferred_element_type=F32)
            y += y_inter * jnp.exp2(jnp.where(lo, col0, col1))
            xw = xt_scr[cols, pl.ds(r0, CHUNK)] * jnp.where(upper, w_r[k0:k0 + 1, :], w_r[k1:k1 + 1, :])
            s_new = jnp.dot(xw.astype(BF16), bc, preferred_element_type=F32)
            h_scr[d, cols, :] = jnp.where(upper, tot_r[k0:k0 + 1, :], tot_r[k1:k1 + 1, :]) * hp + s_new
            if d == 0:
                yf_scr[pl.ds(r0, CHUNK), cols] += y
            else:
                yb_scr[pl.ds(r0, CHUNK), cols] = y

    if has_h0:
        h_scr[...] = h0_ref[...]
    else:
        h_scr[...] = jnp.zeros_like(h_scr)
    lax.fori_loop(0, nc, conv_chunk, 0)

    def scan_body(i, carry):
        chunk_step(i, 0)
        chunk_step(nc - 1 - i, 1)
        return carry

    lax.fori_loop(0, nc, scan_body, 0, unroll=min(nc, 4))

    def finish(c, carry):
        r0 = pl.multiple_of(c * CHUNK, CHUNK)
        y = yf_scr[pl.ds(r0, CHUNK), :] + yb_scr[pl.ds(r0, CHUNK), :]
        g = y * _silu(z_ref[pl.ds(r0, CHUNK), :])
        y_ref[pl.ds(r0, CHUNK), :] = _rms(g, nw_ref[...]).astype(y_ref.dtype)
        return carry

    lax.fori_loop(0, nc, finish, 0, unroll=2)
    if has_ht:
        if n_prev:
            ht_ref[0:n_prev] = prev_ref[...]
        ht_ref[n_prev] = h_scr[...]


def _ssd_scan(proj, acs_c, ldt_r, acs_r, dsk, norm_w, conv_w, conv_b, row0, nseq, seq_len, h0, want_state,
              prev_states=None):
    n_prev = 0 if prev_states is None else prev_states.shape[1]
    groups = acs_c.shape[0]
    gw = SSD_HEADS_PER_GROUP * SSD_HEAD_DIM
    d_inner = groups * gw
    n = SSD_D_STATE
    nc = seq_len // CHUNK
    rb0 = row0 // seq_len
    xoff = d_inner // gw
    boff = 2 * d_inner // n
    coff = boff + groups
    cwx_off = 0
    cwb_off = d_inner // n
    cwc_off = cwb_off + groups

    in_specs = [
        pl.BlockSpec((seq_len, gw), lambda b, g: (rb0 + b, g)),
        pl.BlockSpec((seq_len, gw), lambda b, g: (rb0 + b, xoff + g)),
        pl.BlockSpec((seq_len, n), lambda b, g: (rb0 + b, boff + g)),
        pl.BlockSpec((seq_len, n), lambda b, g: (rb0 + b, coff + g)),
        pl.BlockSpec((None, seq_len, 2 * SSD_HEADS_PER_GROUP), lambda b, g: (g, rb0 + b, 0)),
        pl.BlockSpec((None, 2 * SSD_HEADS_PER_GROUP, seq_len), lambda b, g: (g, 0, rb0 + b)),
        pl.BlockSpec((None, 2 * SSD_HEADS_PER_GROUP, seq_len), lambda b, g: (g, 0, rb0 + b)),
        pl.BlockSpec((1, gw), lambda b, g: (0, g)),
        pl.BlockSpec((1, gw), lambda b, g: (0, g)),
        pl.BlockSpec((SSD_CONV_W, gw), lambda b, g: (0, cwx_off + g)),
        pl.BlockSpec((SSD_CONV_W, n), lambda b, g: (0, cwb_off + g)),
        pl.BlockSpec((SSD_CONV_W, n), lambda b, g: (0, cwc_off + g)),
        pl.BlockSpec((1, gw), lambda b, g: (0, cwx_off + g)),
        pl.BlockSpec((1, n), lambda b, g: (0, cwb_off + g)),
        pl.BlockSpec((1, n), lambda b, g: (0, cwc_off + g)),
    ]
    args = [proj, proj, proj, proj, acs_c, ldt_r, acs_r, dsk, norm_w,
            conv_w, conv_w, conv_w, conv_b, conv_b, conv_b]
    if h0 is not None:
        in_specs.append(pl.BlockSpec((None, 2, gw, n), lambda b, g: (b, 0, g, 0)))
        args.append(h0)
    if n_prev:
        in_specs.append(pl.BlockSpec((None, n_prev, 2, gw, n), lambda b, g: (b, 0, 0, g, 0)))
        args.append(prev_states)
    out_shape = [jax.ShapeDtypeStruct((nseq * seq_len, d_inner), BF16)]
    out_specs = [pl.BlockSpec((seq_len, gw), lambda b, g: (b, g))]
    if want_state:
        out_shape.append(jax.ShapeDtypeStruct((nseq, n_prev + 1, 2, d_inner, n), F32))
        out_specs.append(pl.BlockSpec((None, n_prev + 1, 2, gw, n), lambda b, g: (b, 0, 0, g, 0)))
    outs = pl.pallas_call(
        functools.partial(_ssd_kernel, nc, h0 is not None, want_state, n_prev),
        out_shape=out_shape,
        grid=(nseq, groups),
        in_specs=in_specs,
        out_specs=out_specs,
        scratch_shapes=[
            pltpu.VMEM((seq_len, gw), BF16),
            pltpu.VMEM((gw, seq_len), F32),
            pltpu.VMEM((seq_len, n), BF16),
            pltpu.VMEM((seq_len, n), BF16),
            pltpu.VMEM((seq_len, gw), F32),
            pltpu.VMEM((seq_len, gw), F32),
            pltpu.VMEM((2, gw, n), F32),
        ],
        compiler_params=_params("parallel", "parallel"),
        name="ssd_scan",
    )(*args)
    return outs if want_state else (outs[0], None)


def _ssd_mixer(x, norm_w_mix, mod5, layer, tok, geo, h0_sample, prev_states, j, w_in, conv_w, conv_b, dt_bias,
               a_log, d_skip, norm_w, w_out):
    heads = dt_bias.shape[1]
    groups = heads // SSD_HEADS_PER_GROUP
    d_inner = heads * SSD_HEAD_DIM
    hpg = SSD_HEADS_PER_GROUP
    t = x.shape[0]
    proj = _norm_mod_matmul(x, norm_w_mix, mod5, layer, tok, w_in, j)
    dt_off = d_inner + conv_w.shape[1]
    ldt, acs = _ssd_dt(proj[:, dt_off:dt_off + 2 * heads], dt_bias, a_log)

    def per_group(v):
        return v.reshape(t, 2, groups, hpg).transpose(2, 0, 1, 3).reshape(groups, t, 2 * hpg)

    acs_c = per_group(acs)
    acs_r = acs_c.transpose(0, 2, 1)
    ldt_r = per_group(ldt).transpose(0, 2, 1)
    dsk = jnp.repeat(d_skip, SSD_HEAD_DIM).reshape(1, d_inner)
    nw = norm_w.reshape(1, d_inner)
    cb = conv_b.reshape(1, -1)
    bp, lp, bs, ls = geo
    y_p, states = _ssd_scan(proj, acs_c, ldt_r, acs_r, dsk, nw, conv_w, cb, 0, bp, lp, None, True, prev_states)
    h0 = h0_sample.reshape(bs, 2, d_inner, SSD_D_STATE)
    y_s, _ = _ssd_scan(proj, acs_c, ldt_r, acs_r, dsk, nw, conv_w, cb, bp * lp, bs, ls, h0, False)
    x = _matmul_residual(y_p, y_s, w_out, j, x, mod5, layer, tok)
    return x, states


def _rope(x, cos, sin):
    quarter = ATTN_HEAD_DIM // 4
    lane = lax.broadcasted_iota(jnp.int32, (1, ATTN_HEAD_DIM), 1)
    first = (lane % (2 * quarter)) < quarter
    swapped = jnp.where(first, pltpu.roll(x, ATTN_HEAD_DIM - quarter, 1), pltpu.roll(x, quarter, 1))
    return x * cos + swapped * sin


def _attn_kernel(nq, has_cache, use_rope, write_k, *refs):
    q_ref, k_ref, v_ref, qn_ref, kn_ref = refs[:5]
    pos = 5
    cos_ref = sin_ref = kc_ref = vc_ref = knew_ref = None
    if use_rope:
        cos_ref, sin_ref = refs[pos:pos + 2]
        pos += 2
    if has_cache:
        kc_ref, vc_ref = refs[pos:pos + 2]
        pos += 2
    o_ref = refs[pos]
    pos += 1
    if write_k:
        knew_ref = refs[pos]
        pos += 1
    kb_scr, vt_scr = refs[pos:pos + 2]
    pos += 2
    if has_cache:
        kcb_scr, vct_scr = refs[pos:pos + 2]
        kcb_scr[...] = kc_ref[...].astype(BF16)
        for c in range(vc_ref.shape[0] // CHUNK):
            rows = slice(c * CHUNK, (c + 1) * CHUNK)
            vct_scr[:, rows] = vc_ref[rows, :].T.astype(BF16)

    grp = q_ref.shape[1] // ATTN_HEAD_DIM
    scale = ATTN_HEAD_DIM ** -0.5

    def kprep(c, carry):
        r0 = pl.multiple_of(c * CHUNK, CHUNK)
        kn = _rms(k_ref[pl.ds(r0, CHUNK), :], kn_ref[...])
        if write_k:
            knew_ref[pl.ds(r0, CHUNK), :] = kn
        if use_rope:
            kn = _rope(kn, cos_ref[pl.ds(r0, CHUNK), :], sin_ref[pl.ds(r0, CHUNK), :])
        kb_scr[pl.ds(r0, CHUNK), :] = kn.astype(BF16)
        vt_scr[:, pl.ds(r0, CHUNK)] = v_ref[pl.ds(r0, CHUNK), :].T.astype(BF16)
        return carry

    lax.fori_loop(0, nq, kprep, 0)

    def qblock(c, carry):
        r0 = pl.multiple_of(c * CHUNK, CHUNK)
        q_t = []
        for g in range(grp):
            cols = slice(ATTN_HEAD_DIM * g, ATTN_HEAD_DIM * (g + 1))
            qn = _rms(q_ref[pl.ds(r0, CHUNK), cols], qn_ref[...])
            if use_rope:
                qn = _rope(qn, cos_ref[pl.ds(r0, CHUNK), :], sin_ref[pl.ds(r0, CHUNK), :])
            q_t.append((qn * (scale * LOG2_E)).T.astype(BF16))
        q_t = jnp.concatenate(q_t, axis=1)
        s = jnp.dot(kb_scr[...], q_t, preferred_element_type=F32)
        m = jnp.max(s, axis=0, keepdims=True)
        if has_cache:
            sc = jnp.dot(kcb_scr[...], q_t, preferred_element_type=F32)
            m = jnp.maximum(m, jnp.max(sc, axis=0, keepdims=True))
            ec = jnp.exp2(sc - m)
        e = jnp.exp2(s - m)
        den = jnp.sum(e, axis=0, keepdims=True)
        if has_cache:
            den = den + jnp.sum(ec, axis=0, keepdims=True)
        o_t = jnp.dot(vt_scr[...], e.astype(BF16), preferred_element_type=F32)
        if has_cache:
            o_t += jnp.dot(vct_scr[...], ec.astype(BF16), preferred_element_type=F32)
        o_t = o_t * (1.0 / den)
        for g in range(grp):
            cols = slice(ATTN_HEAD_DIM * g, ATTN_HEAD_DIM * (g + 1))
            o_ref[pl.ds(r0, CHUNK), cols] = o_t[:, cols].T.astype(o_ref.dtype)
        return carry

    lax.fori_loop(0, nq, qblock, 0, unroll=2)


def _attention(qkv, q_norm, k_norm, heads, kv_heads, row0, nseq, seq_len, rope, cache, write_k):
    dh = ATTN_HEAD_DIM
    grp = heads // kv_heads
    rb0 = row0 // seq_len
    nq = seq_len // CHUNK
    in_specs = [
        pl.BlockSpec((seq_len, grp * dh), lambda b, h: (rb0 + b, h)),
        pl.BlockSpec((seq_len, dh), lambda b, h: (rb0 + b, heads + h)),
        pl.BlockSpec((seq_len, dh), lambda b, h: (rb0 + b, heads + kv_heads + h)),
        pl.BlockSpec((1, dh), lambda b, h: (0, 0)),
        pl.BlockSpec((1, dh), lambda b, h: (0, 0)),
    ]
    args = [qkv, qkv, qkv, q_norm.reshape(1, dh), k_norm.reshape(1, dh)]
    scratch = [pltpu.VMEM((seq_len, dh), BF16), pltpu.VMEM((dh, seq_len), BF16)]
    if rope is not None:
        in_specs += [pl.BlockSpec((seq_len, dh), lambda b, h: (0, 0))] * 2
        args += list(rope)
    if cache is not None:
        past = cache[0].shape[1]
        in_specs += [pl.BlockSpec((None, past, dh), lambda b, h: (b, 0, h))] * 2
        args += list(cache)
        scratch += [pltpu.VMEM((past, dh), BF16), pltpu.VMEM((dh, past), BF16)]
    out_shape = [jax.ShapeDtypeStruct((nseq * seq_len, heads * dh), BF16)]
    out_specs = [pl.BlockSpec((seq_len, grp * dh), lambda b, h: (b, h))]
    if write_k:
        out_shape.append(jax.ShapeDtypeStruct((nseq * seq_len, kv_heads * dh), F32))
        out_specs.append(pl.BlockSpec((seq_len, dh), lambda b, h: (b, h)))
    outs = pl.pallas_call(
        functools.partial(_attn_kernel, nq, cache is not None, rope is not None, write_k),
        out_shape=out_shape,
        grid=(nseq, kv_heads),
        in_specs=in_specs,
        out_specs=out_specs,
        scratch_shapes=scratch,
        compiler_params=_params("parallel", "parallel"),
        name="attention",
    )(*args)
    return outs if write_k else (outs[0], None)


def _rope_tables(seq_len):
    quarter = ATTN_HEAD_DIM // 4
    t = jnp.arange(seq_len)
    lane = jnp.arange(ATTN_HEAD_DIM)
    pos = jnp.where(lane[None, :] < 2 * quarter, (t // GRID_W)[:, None], (t % GRID_W)[:, None]).astype(F32)
    inv_freq = ROPE_THETA ** (-jnp.arange(quarter, dtype=F32) / quarter)
    ang = pos * inv_freq[lane % quarter][None, :]
    sign = jnp.where((lane % (2 * quarter)) < quarter, -1.0, 1.0).astype(F32)
    return jnp.cos(ang), jnp.sin(ang) * sign[None, :]


def _attn_mixer(x, norm_w_mix, mod5, layer, tok, geo, cache_k, cache_v, j, w_qkv, q_norm, k_norm, w_out):
    bp, lp, bs, ls = geo
    dh = ATTN_HEAD_DIM
    kv_heads = cache_k.shape[2]
    heads = w_out.shape[1] // dh
    qkv = _norm_mod_matmul(x, norm_w_mix, mod5, layer, tok, w_qkv, j)
    o_p, k_new = _attention(qkv, q_norm, k_norm, heads, kv_heads, 0, bp, lp, None, None, True)
    past = cache_k.shape[1]
    cache = (cache_k.reshape(bs, past, kv_heads * dh), cache_v.reshape(bs, past, kv_heads * dh))
    o_s, _ = _attention(qkv, q_norm, k_norm, heads, kv_heads, bp * lp, bs, ls, _rope_tables(ls), cache, False)
    x = _matmul_residual(o_p, o_s, w_out, j, x, mod5, layer, tok)
    v_new = qkv[:bp * lp, (heads + kv_heads) * dh:]
    return x, k_new.reshape(bp, lp, kv_heads, dh), v_new.reshape(bp, lp, kv_heads, dh)


def _mlstm_gate_kernel(heads, i_ref, f_ref, bi_ref, bf_ref, b_ref, r_ref, p_ref):
    forward = lax.broadcasted_iota(jnp.int32, (1, i_ref.shape[1]), 1) < heads
    for c in range(i_ref.shape[0] // CHUNK):
        rows = slice(c * CHUNK, (c + 1) * CHUNK)
        log_f = _log_sigmoid(f_ref[rows, :] + bf_ref[...])
        b = jnp.where(forward, _cumsum_chunk(log_f, False), _cumsum_chunk(log_f, True))
        r = i_ref[rows, :] + bi_ref[...] - b
        b_ref[rows, :] = b
        r_ref[rows, :] = r
        p_ref[rows, :] = jnp.where(forward, _cummax_chunk(r, False), _cummax_chunk(r, True))


def _mlstm_gates(i_raw, f_raw, b_i, b_f):
    t, cols = i_raw.shape
    tm = _pick_tile(t, (1024, 512, 256, 128))
    return pl.pallas_call(
        functools.partial(_mlstm_gate_kernel, cols // 2),
        out_shape=[jax.ShapeDtypeStruct((t, cols), F32)] * 3,
        grid=(t // tm,),
        in_specs=[
            pl.BlockSpec((tm, cols), lambda i: (i, 0)),
            pl.BlockSpec((tm, cols), lambda i: (i, 0)),
            pl.BlockSpec((1, cols), lambda i: (0, 0)),
            pl.BlockSpec((1, cols), lambda i: (0, 0)),
        ],
        out_specs=[pl.BlockSpec((tm, cols), lambda i: (i, 0))] * 3,
        compiler_params=_params("parallel"),
        name="mlstm_gates",
    )(i_raw, f_raw, b_i, b_f)


def _mlstm_kernel(nc, has_s0, has_st, *refs):
    q_ref, k_ref, v_ref, og_ref, cp_ref, rr_ref, hn_ref = refs[:7]
    pos = 7
    c0_ref = n0_ref = m0_ref = ct_ref = nt_ref = mt_ref = None
    if has_s0:
        c0_ref, n0_ref, m0_ref = refs[pos:pos + 3]
        pos += 3
    h_ref = refs[pos]
    pos += 1
    if has_st:
        ct_ref, nt_ref, mt_ref = refs[pos:pos + 3]
        pos += 3
    hf_scr, hb_scr, c_scr, n_scr, m_scr = refs[pos:]

    dk = q_ref.shape[1]
    dv = v_ref.shape[1]
    qscale = dk ** -0.5
    tile = (CHUNK, CHUNK)
    ti = lax.broadcasted_iota(jnp.int32, tile, 0)
    si = lax.broadcasted_iota(jnp.int32, tile, 1)

    def wide(x):
        return jnp.concatenate([x] * (dv // CHUNK), axis=1)

    def chunk_step(c, d):
        r0 = pl.multiple_of(c * CHUNK, CHUNK)
        q = q_ref[pl.ds(r0, CHUNK), :] * qscale
        k = k_ref[pl.ds(r0, CHUNK), :]
        qb = q.astype(BF16)
        kb = k.astype(BF16)
        vb = v_ref[pl.ds(r0, CHUNK), :].astype(BF16)
        cp = cp_ref[pl.ds(r0, CHUNK), :]
        b_t = jnp.broadcast_to(cp[:, 3 * d:3 * d + 1], tile)
        r_t = jnp.broadcast_to(cp[:, 3 * d + 1:3 * d + 2], tile)
        p_t = jnp.broadcast_to(cp[:, 3 * d + 2:3 * d + 3], tile)
        r_row = rr_ref[d:d + 1, pl.ds(r0, CHUNK)]
        m_row = m_scr[d]
        u_t = jnp.maximum(m_row, p_t)
        mask = (ti >= si) if d == 0 else (ti <= si)
        s = lax.dot_general(qb, kb, NT_DIMS, preferred_element_type=F32)
        s = s * jnp.exp(jnp.where(mask, r_row - u_t, -jnp.inf))
        inter = jnp.exp(m_row - u_t)
        cst = c_scr[d]
        nst = n_scr[d]
        num = jnp.dot(s.astype(BF16), vb, preferred_element_type=F32)
        num += wide(inter) * jnp.dot(qb, cst.astype(BF16), preferred_element_type=F32)
        den = jnp.sum(s + inter * (q * nst), axis=1, keepdims=True)
        rcp = 1.0 / jnp.maximum(jnp.abs(den), jnp.exp(-(b_t + u_t)))
        h = num * wide(rcp)
        last = CHUNK - 1 if d == 0 else 0
        u_last = u_t[last:last + 1, :]
        wk = jnp.exp(r_t - u_last) * k
        carry_decay = jnp.exp(m_row - u_last)
        c_scr[d] = wide(carry_decay) * cst + jnp.dot(wk.T.astype(BF16), vb, preferred_element_type=F32)
        n_scr[d] = carry_decay * nst + jnp.sum(wk, axis=0, keepdims=True)
        m_scr[d] = b_t[last:last + 1, :] + u_last
        (hf_scr if d == 0 else hb_scr)[pl.ds(r0, CHUNK), :] = h

    for d in range(2):
        if has_s0:
            c_scr[d] = c0_ref[d]
            n_scr[d] = n0_ref[d]
            m_scr[d] = m0_ref[d]
        else:
            c_scr[d] = jnp.zeros(c_scr.shape[1:], F32)
            n_scr[d] = jnp.zeros(n_scr.shape[1:], F32)
            m_scr[d] = jnp.zeros(m_scr.shape[1:], F32)

    def scan_body(i, carry):
        chunk_step(i, 0)
        chunk_step(nc - 1 - i, 1)
        return carry

    lax.fori_loop(0, nc, scan_body, 0, unroll=2)

    def finish(c, carry):
        r0 = pl.multiple_of(c * CHUNK, CHUNK)
        hs = _rms(hf_scr[pl.ds(r0, CHUNK), :] + hb_scr[pl.ds(r0, CHUNK), :], hn_ref[...])
        h_ref[pl.ds(r0, CHUNK), :] = (hs * _sigmoid(og_ref[pl.ds(r0, CHUNK), :])).astype(h_ref.dtype)
        return carry

    lax.fori_loop(0, nc, finish, 0, unroll=2)
    if has_st:
        for d in range(2):
            ct_ref[d] = c_scr[d]
            nt_ref[d] = n_scr[d]
            mt_ref[d] = m_scr[d]


def _mlstm_scan(proj, gate_cols, gate_rows, head_norm, heads, dk, dv, row0, nseq, seq_len, state, want_state):
    assert dv == 2 * dk and dk == CHUNK
    nc = seq_len // CHUNK
    rb0 = row0 // seq_len
    in_specs = [
        pl.BlockSpec((seq_len, dk), lambda b, h: (rb0 + b, h)),
        pl.BlockSpec((seq_len, dk), lambda b, h: (rb0 + b, heads + h)),
        pl.BlockSpec((seq_len, dv), lambda b, h: (rb0 + b, heads + h)),
        pl.BlockSpec((seq_len, dv), lambda b, h: (rb0 + b, 2 * heads + h)),
        pl.BlockSpec((None, seq_len, 6), lambda b, h: (h, rb0 + b, 0)),
        pl.BlockSpec((None, 2, seq_len), lambda b, h: (h, 0, rb0 + b)),
        pl.BlockSpec((None, 1, dv), lambda b, h: (h, 0, 0)),
    ]
    args = [proj, proj, proj, proj, gate_cols, gate_rows, head_norm]
    state_specs = [
        pl.BlockSpec((None, 2, None, dk, dv), lambda b, h: (b, 0, h, 0, 0)),
        pl.BlockSpec((None, 2, None, 1, dk), lambda b, h: (b, 0, h, 0, 0)),
        pl.BlockSpec((None, 2, None, 1, LANES), lambda b, h: (b, 0, h, 0, 0)),
    ]
    if state is not None:
        in_specs += state_specs
        args += list(state)
    out_shape = [jax.ShapeDtypeStruct((nseq * seq_len, heads * dv), BF16)]
    out_specs = [pl.BlockSpec((seq_len, dv), lambda b, h: (b, h))]
    if want_state:
        out_shape += [jax.ShapeDtypeStruct((nseq, 2, heads, dk, dv), F32),
                      jax.ShapeDtypeStruct((nseq, 2, heads, 1, dk), F32),
                      jax.ShapeDtypeStruct((nseq, 2, heads, 1, LANES), F32)]
        out_specs += state_specs
    outs = pl.pallas_call(
        functools.partial(_mlstm_kernel, nc, state is not None, want_state),
        out_shape=out_shape,
        grid=(nseq, heads),
        in_specs=in_specs,
        out_specs=out_specs,
        scratch_shapes=[
            pltpu.VMEM((seq_len, dv), F32),
            pltpu.VMEM((seq_len, dv), F32),
            pltpu.VMEM((2, dk, dv), F32),
            pltpu.VMEM((2, 1, dk), F32),
            pltpu.VMEM((2, 1, LANES), F32),
        ],
        compiler_params=_params("parallel", "parallel"),
        name="mlstm_scan",
    )(*args)
    return outs


def _mlstm_mixer(x, norm_w_mix, mod5, layer, tok, geo, c0, n0, m0, j, w_in, b_gates, head_norm, w_out):
    bp, lp, bs, ls = geo
    heads, dk, dv = c0.shape[2], c0.shape[3], c0.shape[4]
    t = x.shape[0]
    proj = _norm_mod_matmul(x, norm_w_mix, mod5, layer, tok, w_in, j)
    g_off = 2 * heads * dk + 2 * heads * dv
    graw = proj[:, g_off:g_off + 4 * heads].reshape(t, 2, 2, heads)
    b, r, p = _mlstm_gates(graw[:, :, 0, :].reshape(t, 2 * heads), graw[:, :, 1, :].reshape(t, 2 * heads),
                           b_gates[:, 0, :].reshape(1, 2 * heads), b_gates[:, 1, :].reshape(1, 2 * heads))
    gate_cols = jnp.stack([b, r, p], axis=-1).reshape(t, 2, heads, 3).transpose(2, 0, 1, 3).reshape(heads, t, 6)
    gate_rows = r.reshape(t, 2, heads).transpose(2, 1, 0)
    hn = head_norm.reshape(heads, 1, dv)
    outs_p = _mlstm_scan(proj, gate_cols, gate_rows, hn, heads, dk, dv, 0, bp, lp, None, True)
    h_p, c_t, n_t, m_t = outs_p
    state = (c0, n0.reshape(bs, 2, heads, 1, dk),
             jnp.broadcast_to(m0[..., None, None], (bs, 2, heads, 1, LANES)))
    (h_s,) = _mlstm_scan(proj, gate_cols, gate_rows, hn, heads, dk, dv, bp * lp, bs, ls, state, False)
    x = _matmul_residual(h_p, h_s, w_out, j, x, mod5, layer, tok)
    return x, c_t, n_t.reshape(bp, 2, heads, dk), m_t[:, :, :, 0, 0]


def kernel(x_prompt, x_sample, c, state_ssd, cache_attn_k, cache_attn_v, state_mlstm_C, state_mlstm_n, state_mlstm_m, c_ctx, ada_w, ada_b, norm_mix_w, norm_ffn_w, ffn_w_gate, ffn_w_up, ffn_w_down, ssd_w_in, ssd_conv_w, ssd_conv_b, ssd_dt_bias, ssd_a_log, ssd_d, ssd_norm_w, ssd_w_out, attn_w_qkv, attn_q_norm, attn_k_norm, attn_w_out, mlstm_w_in, mlstm_b_gates, mlstm_head_norm, mlstm_w_out):
    bp, lp, d = x_prompt.shape
    bs, ls, _ = x_sample.shape
    depth = ada_w.shape[0]
    geo = (bp, lp, bs, ls)
    tok = _Tokens(bp * lp, ls)
    x = jnp.concatenate([x_prompt.reshape(bp * lp, d), x_sample.reshape(bs * ls, d)], axis=0)

    mod_rows = -(-(1 + bs) // SUBLANES) * SUBLANES
    cc = jnp.concatenate([c_ctx[None, :], c, jnp.zeros((mod_rows - 1 - bs, d), F32)], axis=0)
    mod5 = _modulation(cc, ada_w, ada_b).reshape(depth, mod_rows, 6, 1, d)

    w_gate, w_up, w_down = _to_bf16(ffn_w_gate), _to_bf16(ffn_w_up), _to_bf16(ffn_w_down)
    ssd_in, ssd_out = _to_bf16(ssd_w_in, 6 * MXU_COLS), _to_bf16(ssd_w_out)
    attn_qkv, attn_out = _to_bf16(attn_w_qkv), _to_bf16(attn_w_out)
    mlstm_in = _to_bf16(mlstm_w_in, 5 * MXU_COLS)
    mlstm_out = _to_bf16(mlstm_w_out)
    ssd_states = None
    new_k, new_v, new_c, new_n, new_m = [], [], [], [], []
    for l in range(depth):
        kind, j = l % 3, l // 3
        if kind == 0:
            x, ssd_states = _ssd_mixer(x, norm_mix_w[l], mod5, l, tok, geo, state_ssd[:, j], ssd_states, j, ssd_in,
                                       ssd_conv_w[j], ssd_conv_b[j], ssd_dt_bias[j], ssd_a_log[j], ssd_d[j],
                                       ssd_norm_w[j], ssd_out)
        elif kind == 1:
            x, k_new, v_new = _attn_mixer(x, norm_mix_w[l], mod5, l, tok, geo, cache_attn_k[:, j], cache_attn_v[:, j],
                                          j, attn_qkv, attn_q_norm[j], attn_k_norm[j], attn_out)
            new_k.append(k_new)
            new_v.append(v_new)
        else:
            x, c_t, n_t, m_t = _mlstm_mixer(x, norm_mix_w[l], mod5, l, tok, geo, state_mlstm_C[:, j],
                                            state_mlstm_n[:, j], state_mlstm_m[:, j], j, mlstm_in,
                                            mlstm_b_gates[j], mlstm_head_norm[j], mlstm_out)
            new_c.append(c_t)
            new_n.append(n_t)
            new_m.append(m_t)
        x = _ffn(x, norm_ffn_w[l], mod5, l, tok, w_gate, w_up, w_down, split_out=(l == depth - 1))

    y_prompt = x[0].reshape(bp, lp, d)
    y_sample = x[1].reshape(bs, ls, d)
    ssd_states = ssd_states.reshape((bp,) + state_ssd.shape[1:])
    return (y_prompt, y_sample, ssd_states, jnp.stack(new_k, axis=1), jnp.stack(new_v, axis=1),
            jnp.stack(new_c, axis=1), jnp.stack(new_n, axis=1), jnp.stack(new_m, axis=1))
```

```python
import functools

import jax
import jax.numpy as jnp
from jax import lax
from jax.experimental import pallas as pl
from jax.experimental.pallas import tpu as pltpu

F32 = jnp.float32
BF16 = jnp.bfloat16

EPS = 1e-6
CHUNK = 128
GRID_W = 64
ROPE_THETA = 10000.0
SSD_HEAD_DIM = 64
SSD_D_STATE = 128
SSD_CONV_W = 3
SSD_HEADS_PER_GROUP = 8
ATTN_HEAD_DIM = 128
LANES = 128
SUBLANES = 8
MXU_COLS = 256
NORM_ROWS = 16
VMEM_LIMIT_BYTES = 56 * 1024 * 1024

NT_DIMS = (((1,), (1,)), ((), ()))
LOG2_E = 1.4426950408889634


def _sigmoid(x):
    return 0.5 + 0.5 * jnp.tanh(0.5 * x)


def _silu(x):
    h = 0.5 * x
    return h + h * jnp.tanh(h)


def _softplus(x):
    return jnp.maximum(x, 0.0) + jnp.log1p(jnp.exp(-jnp.abs(x)))


def _log_sigmoid(x):
    return -_softplus(-x)


def _rms(x, w):
    return x * lax.rsqrt(jnp.mean(x * x, axis=-1, keepdims=True) + EPS) * w


def _params(*sem):
    return pltpu.CompilerParams(dimension_semantics=sem, vmem_limit_bytes=VMEM_LIMIT_BYTES)


def _pick_tile(n, candidates):
    for c in candidates:
        if n % c == 0:
            return c
    return n


def _scan_chunk(a, reverse, combine, identity):
    idx = lax.broadcasted_iota(jnp.int32, (CHUNK, 1), 0)
    sh = 1
    while sh < CHUNK:
        if reverse:
            a = combine(a, jnp.where(idx < CHUNK - sh, pltpu.roll(a, CHUNK - sh, 0), identity))
        else:
            a = combine(a, jnp.where(idx >= sh, pltpu.roll(a, sh, 0), identity))
        sh *= 2
    return a


def _cumsum_chunk(a, reverse):
    return _scan_chunk(a, reverse, jnp.add, 0.0)


def _cummax_chunk(a, reverse):
    return _scan_chunk(a, reverse, jnp.maximum, -jnp.inf)


def _cast_kernel(w_ref, o_ref):
    n = w_ref.shape[1]
    o_ref[:, :n] = w_ref[...].astype(BF16)
    if o_ref.shape[1] > n:
        o_ref[:, n:] = jnp.zeros((o_ref.shape[0], o_ref.shape[1] - n), BF16)


def _to_bf16(w, col_multiple=1):
    shape = w.shape
    w2 = w.reshape(-1, shape[-1])
    rows, n = w2.shape
    n_out = -(-n // col_multiple) * col_multiple
    if n_out != n and n % LANES:
        return jnp.pad(_to_bf16(w), [(0, 0)] * (len(shape) - 1) + [(0, n_out - n)])
    tr = _pick_tile(rows, (256, 128, 64, 32, 16))
    out = pl.pallas_call(
        _cast_kernel,
        out_shape=jax.ShapeDtypeStruct((rows, n_out), BF16),
        grid=(rows // tr,),
        in_specs=[pl.BlockSpec((tr, n), lambda i: (i, 0))],
        out_specs=pl.BlockSpec((tr, n_out), lambda i: (i, 0)),
        compiler_params=_params("parallel"),
        name="cast_bf16",
    )(w2)
    return out.reshape(shape[:-1] + (n_out,))


def _mod_kernel(c_ref, w_ref, b_ref, o_ref):
    s = _silu(c_ref[...]).astype(BF16)
    o_ref[...] = jnp.dot(s, w_ref[...].astype(BF16), preferred_element_type=F32) + b_ref[...]


def _modulation(cc, ada_w, ada_b):
    depth, d, n = ada_w.shape
    rows = cc.shape[0]
    tn = _pick_tile(n, (1024, 512, 256, 128))
    return pl.pallas_call(
        _mod_kernel,
        out_shape=jax.ShapeDtypeStruct((depth, rows, n), F32),
        grid=(depth, n // tn),
        in_specs=[
            pl.BlockSpec((rows, d), lambda l, j: (0, 0)),
            pl.BlockSpec((None, d, tn), lambda l, j: (l, 0, j)),
            pl.BlockSpec((None, 1, tn), lambda l, j: (l, 0, j)),
        ],
        out_specs=pl.BlockSpec((None, rows, tn), lambda l, j: (l, 0, j)),
        compiler_params=_params("parallel", "parallel"),
        name="adaln_mod",
    )(cc, ada_w, ada_b.reshape(depth, 1, n))


class _Tokens:
    def __init__(self, n_prompt_tok, dec_seq):
        self.n_prompt_tok = n_prompt_tok
        self.dec_seq = dec_seq

    def row(self, start):
        return jnp.where(start < self.n_prompt_tok, 0, 1 + (start - self.n_prompt_tok) // self.dec_seq)

    def tile(self, candidates):
        for c in candidates:
            if self.n_prompt_tok % c == 0 and self.dec_seq % c == 0:
                return c
        raise ValueError("no token tile fits the sequence layout")


def _mod_spec(tok, tm, layer, k, d, grid_pos):
    def index_map(*ids):
        return (layer, tok.row(ids[grid_pos] * tm), k, 0, 0)
    return pl.BlockSpec((None, None, None, 1, d), index_map)


def _norm_mod(u_ref, x_ref, nw_ref, sh_ref, sc_ref):
    gain = nw_ref[...] * (1.0 + sc_ref[...])
    shift = sh_ref[...]

    def body(r, carry):
        r0 = pl.multiple_of(r * NORM_ROWS, NORM_ROWS)
        x = x_ref[pl.ds(r0, NORM_ROWS), :]
        inv = lax.rsqrt(jnp.mean(x * x, axis=-1, keepdims=True) + EPS)
        u_ref[pl.ds(r0, NORM_ROWS), :] = (x * inv * gain + shift).astype(BF16)
        return carry

    lax.fori_loop(0, x_ref.shape[0] // NORM_ROWS, body, 0, unroll=4)


def _nmm_kernel(x_ref, nw_ref, sh_ref, sc_ref, w_ref, o_ref, u_ref):
    @pl.when(pl.program_id(1) == 0)
    def _():
        _norm_mod(u_ref, x_ref, nw_ref, sh_ref, sc_ref)

    o_ref[...] = jnp.dot(u_ref[...], w_ref[...], preferred_element_type=F32)


def _norm_mod_matmul(x, norm_w, mod5, layer, tok, w, widx):
    t, d = x.shape
    n = w.shape[2]
    tm = tok.tile((1024, 512, 256, 128))
    tn = _pick_tile(n, (1536, 1280, 1024, 768, 512, 256, 128))
    return pl.pallas_call(
        _nmm_kernel,
        out_shape=jax.ShapeDtypeStruct((t, n), F32),
        grid=(t // tm, n // tn),
        in_specs=[
            pl.BlockSpec((tm, d), lambda i, j: (i, 0)),
            pl.BlockSpec((1, d), lambda i, j: (0, 0)),
            _mod_spec(tok, tm, layer, 0, d, 0),
            _mod_spec(tok, tm, layer, 1, d, 0),
            pl.BlockSpec((None, d, tn), lambda i, j: (widx, 0, j)),
        ],
        out_specs=pl.BlockSpec((tm, tn), lambda i, j: (i, j)),
        scratch_shapes=[pltpu.VMEM((tm, d), BF16)],
        compiler_params=_params("parallel", "arbitrary"),
        name="norm_mod_matmul",
    )(x, norm_w.reshape(1, d), mod5, mod5, w)


def _mmres_kernel(prompt_tiles, yp_ref, ys_ref, w_ref, x_ref, g_ref, o_ref):
    def run(y_ref):
        o_ref[...] = x_ref[...] + g_ref[...] * jnp.dot(y_ref[...], w_ref[...], preferred_element_type=F32)

    is_prompt = pl.program_id(1) < prompt_tiles
    pl.when(is_prompt)(lambda: run(yp_ref))
    pl.when(jnp.logical_not(is_prompt))(lambda: run(ys_ref))


def _matmul_residual(y_prompt, y_sample, w, widx, x, mod5, layer, tok):
    kdim = y_prompt.shape[1]
    t, d = x.shape
    tm = tok.tile((512, 256, 128))
    tn = _pick_tile(d, (1024, 512, 256, 128))
    nj = d // tn
    prompt_tiles = tok.n_prompt_tok // tm

    def gate_map(j, i):
        return (layer, tok.row(i * tm), 2, 0, j)

    return pl.pallas_call(
        functools.partial(_mmres_kernel, prompt_tiles),
        out_shape=jax.ShapeDtypeStruct((t, d), F32),
        grid=(nj, t // tm),
        in_specs=[
            pl.BlockSpec((tm, kdim), lambda j, i: (jnp.minimum(i, prompt_tiles - 1), 0)),
            pl.BlockSpec((tm, kdim), lambda j, i: (jnp.maximum(i - prompt_tiles, 0), 0)),
            pl.BlockSpec((None, kdim, tn), lambda j, i: (widx, 0, j)),
            pl.BlockSpec((tm, tn), lambda j, i: (i, j)),
            pl.BlockSpec((None, None, None, 1, tn), gate_map),
        ],
        out_specs=pl.BlockSpec((tm, tn), lambda j, i: (i, j)),
        compiler_params=_params("parallel", "parallel"),
        name="matmul_residual",
    )(y_prompt, y_sample, w, x, mod5)


def _ffn_kernel(prompt_tiles, x_ref, nw_ref, sh_ref, sc_ref, g_ref, wg_ref, wu_ref, wd_ref, *refs):
    u_ref, acc_ref = refs[-2:]
    j = pl.program_id(1)

    @pl.when(j == 0)
    def _():
        _norm_mod(u_ref, x_ref, nw_ref, sh_ref, sc_ref)
        acc_ref[...] = jnp.zeros_like(acc_ref)

    u = u_ref[...]
    a = jnp.dot(u, wg_ref[...], preferred_element_type=F32)
    b = jnp.dot(u, wu_ref[...], preferred_element_type=F32)
    h = (_silu(a) * b).astype(BF16)
    acc_ref[...] += jnp.dot(h, wd_ref[...], preferred_element_type=F32)

    last = j == pl.num_programs(1) - 1
    if prompt_tiles is None:
        (o_ref,) = refs[:-2]

        @pl.when(last)
        def _():
            o_ref[...] = x_ref[...] + g_ref[...] * acc_ref[...]
    else:
        op_ref, os_ref = refs[:-2]
        is_prompt = pl.program_id(0) < prompt_tiles

        @pl.when(jnp.logical_and(last, is_prompt))
        def _():
            op_ref[...] = x_ref[...] + g_ref[...] * acc_ref[...]

        @pl.when(jnp.logical_and(last, jnp.logical_not(is_prompt)))
        def _():
            os_ref[...] = x_ref[...] + g_ref[...] * acc_ref[...]


def _ffn(x, norm_w, mod5, layer, tok, wg, wu, wd, split_out=False):
    t, d = x.shape
    hid = wg.shape[2]
    tm = tok.tile((512, 256, 128))
    th = _pick_tile(hid, (512, 256, 128))
    if split_out:
        prompt_tiles = tok.n_prompt_tok // tm
        out_shape = [jax.ShapeDtypeStruct((tok.n_prompt_tok, d), F32),
                     jax.ShapeDtypeStruct((t - tok.n_prompt_tok, d), F32)]
        out_specs = [pl.BlockSpec((tm, d), lambda i, j: (jnp.minimum(i, prompt_tiles - 1), 0)),
                     pl.BlockSpec((tm, d), lambda i, j: (jnp.maximum(i - prompt_tiles, 0), 0))]
    else:
        prompt_tiles = None
        out_shape = jax.ShapeDtypeStruct((t, d), F32)
        out_specs = pl.BlockSpec((tm, d), lambda i, j: (i, 0))
    return pl.pallas_call(
        functools.partial(_ffn_kernel, prompt_tiles),
        out_shape=out_shape,
        grid=(t // tm, hid // th),
        in_specs=[
            pl.BlockSpec((tm, d), lambda i, j: (i, 0)),
            pl.BlockSpec((1, d), lambda i, j: (0, 0)),
            _mod_spec(tok, tm, layer, 3, d, 0),
            _mod_spec(tok, tm, layer, 4, d, 0),
            _mod_spec(tok, tm, layer, 5, d, 0),
            pl.BlockSpec((None, d, th), lambda i, j: (layer, 0, j)),
            pl.BlockSpec((None, d, th), lambda i, j: (layer, 0, j)),
            pl.BlockSpec((None, th, d), lambda i, j: (layer, j, 0)),
        ],
        out_specs=out_specs,
        scratch_shapes=[pltpu.VMEM((tm, d), BF16), pltpu.VMEM((tm, d), F32)],
        compiler_params=_params("arbitrary" if split_out else "parallel", "arbitrary"),
        name="ffn",
    )(x, norm_w.reshape(1, d), mod5, mod5, mod5, wg, wu, wd)


def _ssd_dt_kernel(heads, raw_ref, bias_ref, alog_ref, acs_ref, acst_ref, ldtt_ref):
    forward = lax.broadcasted_iota(jnp.int32, (1, raw_ref.shape[1]), 1) < heads
    neg_a = -jnp.exp(alog_ref[...])
    for c in range(raw_ref.shape[0] // CHUNK):
        rows = slice(c * CHUNK, (c + 1) * CHUNK)
        dt = _softplus(raw_ref[rows, :] + bias_ref[...])
        a = dt * neg_a
        acs2 = jnp.where(forward, _cumsum_chunk(a, False), _cumsum_chunk(a, True)) * LOG2_E
        acs_ref[rows, :] = acs2
        acst_ref[:, rows] = acs2.T
        ldtt_ref[:, rows] = (jnp.log(dt) * LOG2_E - acs2).T


def _ssd_dt(dt_raw, dt_bias, a_log):
    t, cols = dt_raw.shape
    tm = _pick_tile(t, (1024, 512, 256, 128))
    return pl.pallas_call(
        functools.partial(_ssd_dt_kernel, cols // 2),
        out_shape=[jax.ShapeDtypeStruct((t, cols), F32), jax.ShapeDtypeStruct((cols, t), F32),
                   jax.ShapeDtypeStruct((cols, t), F32)],
        grid=(t // tm,),
        in_specs=[
            pl.BlockSpec((tm, cols), lambda i: (i, 0)),
            pl.BlockSpec((1, cols), lambda i: (0, 0)),
            pl.BlockSpec((1, cols), lambda i: (0, 0)),
        ],
        out_specs=[pl.BlockSpec((tm, cols), lambda i: (i, 0)), pl.BlockSpec((cols, tm), lambda i: (0, i)),
                   pl.BlockSpec((cols, tm), lambda i: (0, i))],
        compiler_params=_params("parallel"),
        name="ssd_dt",
    )(dt_raw, dt_bias.reshape(1, cols), a_log.reshape(1, cols))


def _ssd_kernel(nc, has_h0, has_ht, n_prev, *refs):
    (z_ref, x_ref, b_ref, c_ref, acsc_ref, ldtr_ref, acsr_ref, dsk_ref, nw_ref,
     cwx_ref, cwb_ref, cwc_ref, cbx_ref, cbb_ref, cbc_ref) = refs[:15]
    pos = 15
    h0_ref = None
    ht_ref = None
    prev_ref = None
    if has_h0:
        h0_ref = refs[pos]
        pos += 1
    if n_prev:
        prev_ref = refs[pos]
        pos += 1
    y_ref = refs[pos]
    pos += 1
    if has_ht:
        ht_ref = refs[pos]
        pos += 1
    xb_scr, xt_scr, bs_scr, cs_scr, yf_scr, yb_scr, h_scr = refs[pos:]

    seq_len = nc * CHUNK
    pairs = x_ref.shape[1] // LANES
    row = lax.broadcasted_iota(jnp.int32, (CHUNK, 1), 0)
    upper = row < SSD_HEAD_DIM
    lo = lax.broadcasted_iota(jnp.int32, (1, LANES), 1) < SSD_HEAD_DIM
    ti = lax.broadcasted_iota(jnp.int32, (CHUNK, CHUNK), 0)
    si = lax.broadcasted_iota(jnp.int32, (CHUNK, CHUNK), 1)

    def conv_silu(src, w_ref, bias_ref, c, r0):
        rp = pl.multiple_of(jnp.maximum(r0 - SUBLANES, 0), SUBLANES)
        rn = pl.multiple_of(jnp.minimum(r0 + CHUNK, seq_len - SUBLANES), SUBLANES)
        cur = src[pl.ds(r0, CHUNK), :]
        prow = jnp.where(c > 0, src[pl.ds(rp, SUBLANES), :][SUBLANES - 1:SUBLANES, :], 0.0)
        nrow = jnp.where(c < nc - 1, src[pl.ds(rn, SUBLANES), :][0:1, :], 0.0)
        xm1 = jnp.where(row == 0, prow, pltpu.roll(cur, 1, 0))
        xp1 = jnp.where(row == CHUNK - 1, nrow, pltpu.roll(cur, CHUNK - 1, 0))
        w = w_ref[...]
        return _silu(w[0:1, :] * xm1 + w[1:2, :] * cur + w[2:3, :] * xp1 + bias_ref[...])

    def conv_chunk(c, carry):
        r0 = pl.multiple_of(c * CHUNK, CHUNK)
        xs = conv_silu(x_ref, cwx_ref, cbx_ref, c, r0)
        xb_scr[pl.ds(r0, CHUNK), :] = xs.astype(BF16)
        yf_scr[pl.ds(r0, CHUNK), :] = dsk_ref[...] * xs
        for p in range(pairs):
            cols = slice(LANES * p, LANES * (p + 1))
            xt_scr[cols, pl.ds(r0, CHUNK)] = xs[:, cols].T
        bs_scr[pl.ds(r0, CHUNK), :] = conv_silu(b_ref, cwb_ref, cbb_ref, c, r0).astype(BF16)
        cs_scr[pl.ds(r0, CHUNK), :] = conv_silu(c_ref, cwc_ref, cbc_ref, c, r0).astype(BF16)
        return carry

    def chunk_step(c, d):
        r0 = pl.multiple_of(c * CHUNK, CHUNK)
        xb = xb_scr[pl.ds(r0, CHUNK), :]
        bc = bs_scr[pl.ds(r0, CHUNK), :]
        cc = cs_scr[pl.ds(r0, CHUNK), :]
        acs_c = acsc_ref[pl.ds(r0, CHUNK), :]
        acs_r = acsr_ref[:, pl.ds(r0, CHUNK)]
        ldt_r = ldtr_ref[:, pl.ds(r0, CHUNK)]
        last = CHUNK - 1 if d == 0 else 0
        end_r = jnp.broadcast_to(acs_r[:, last:last + 1], acs_r.shape)
        w_r = jnp.exp2(end_r + ldt_r)
        tot_r = jnp.exp2(end_r)
        cb = lax.dot_general(cc, bc, NT_DIMS, preferred_element_type=F32)
        mask = (ti >= si) if d == 0 else (ti <= si)

        def head(k):
            col = jnp.broadcast_to(acs_c[:, k:k + 1], (CHUNK, CHUNK))
            decay = jnp.exp2(jnp.where(mask, col + ldt_r[k:k + 1, :], -jnp.inf))
            return col, (cb * decay).astype(BF16)

        for p in range(pairs):
            k0 = SSD_HEADS_PER_GROUP * d + 2 * p
            k1 = k0 + 1
            cols = slice(LANES * p, LANES * (p + 1))
            col0, m0 = head(k0)
            col1, m1 = head(k1)
            xbp = xb[:, cols]
            zero = jnp.zeros_like(xbp)
            y = jnp.dot(m0, jnp.where(lo, xbp, zero), preferred_element_type=F32)
            y += jnp.dot(m1, jnp.where(lo, zero, xbp), preferred_element_type=F32)
            hp = h_scr[d, cols, :]
            y_inter = lax.dot_general(cc, hp.astype(BF16), NT_DIMS, preferred_element_type=F32)
            y += y_inter * jnp.exp2(jnp.where(lo, col0, col1))
            xw = xt_scr[cols, pl.ds(r0, CHUNK)] * jnp.where(upper, w_r[k0:k0 + 1, :], w_r[k1:k1 + 1, :])
            s_new = jnp.dot(xw.astype(BF16), bc, preferred_element_type=F32)
            h_scr[d, cols, :] = jnp.where(upper, tot_r[k0:k0 + 1, :], tot_r[k1:k1 + 1, :]) * hp + s_new
            if d == 0:
                yf_scr[pl.ds(r0, CHUNK), cols] += y
            else:
                yb_scr[pl.ds(r0, CHUNK), cols] = y

    if has_h0:
        h_scr[...] = h0_ref[...]
    else:
        h_scr[...] = jnp.zeros_like(h_scr)
    lax.fori_loop(0, nc, conv_chunk, 0, unroll=2)

    def scan_body(i, carry):
        chunk_step(i, 0)
        chunk_step(nc - 1 - i, 1)
        return carry

    lax.fori_loop(0, nc, scan_body, 0, unroll=min(nc, 4))

    def finish(c, carry):
        r0 = pl.multiple_of(c * CHUNK, CHUNK)
        y = yf_scr[pl.ds(r0, CHUNK), :] + yb_scr[pl.ds(r0, CHUNK), :]
        g = y * _silu(z_ref[pl.ds(r0, CHUNK), :])
        y_ref[pl.ds(r0, CHUNK), :] = _rms(g, nw_ref[...]).astype(y_ref.dtype)
        return carry

    lax.fori_loop(0, nc, finish, 0, unroll=2)
    if has_ht:
        if n_prev:
            ht_ref[0:n_prev] = prev_ref[...]
        ht_ref[n_prev] = h_scr[...]


def _ssd_scan(proj, acs_c, ldt_r, acs_r, dsk, norm_w, conv_w, conv_b, row0, nseq, seq_len, h0, want_state,
              prev_states=None):
    n_prev = 0 if prev_states is None else prev_states.shape[1]
    groups = acs_c.shape[0]
    gw = SSD_HEADS_PER_GROUP * SSD_HEAD_DIM
    d_inner = groups * gw
    n = SSD_D_STATE
    nc = seq_len // CHUNK
    rb0 = row0 // seq_len
    xoff = d_inner // gw
    boff = 2 * d_inner // n
    coff = boff + groups
    cwx_off = 0
    cwb_off = d_inner // n
    cwc_off = cwb_off + groups

    in_specs = [
        pl.BlockSpec((seq_len, gw), lambda b, g: (rb0 + b, g)),
        pl.BlockSpec((seq_len, gw), lambda b, g: (rb0 + b, xoff + g)),
        pl.BlockSpec((seq_len, n), lambda b, g: (rb0 + b, boff + g)),
        pl.BlockSpec((seq_len, n), lambda b, g: (rb0 + b, coff + g)),
        pl.BlockSpec((None, seq_len, 2 * SSD_HEADS_PER_GROUP), lambda b, g: (g, rb0 + b, 0)),
        pl.BlockSpec((None, 2 * SSD_HEADS_PER_GROUP, seq_len), lambda b, g: (g, 0, rb0 + b)),
        pl.BlockSpec((None, 2 * SSD_HEADS_PER_GROUP, seq_len), lambda b, g: (g, 0, rb0 + b)),
        pl.BlockSpec((1, gw), lambda b, g: (0, g)),
        pl.BlockSpec((1, gw), lambda b, g: (0, g)),
        pl.BlockSpec((SSD_CONV_W, gw), lambda b, g: (0, cwx_off + g)),
        pl.BlockSpec((SSD_CONV_W, n), lambda b, g: (0, cwb_off + g)),
        pl.BlockSpec((SSD_CONV_W, n), lambda b, g: (0, cwc_off + g)),
        pl.BlockSpec((1, gw), lambda b, g: (0, cwx_off + g)),
        pl.BlockSpec((1, n), lambda b, g: (0, cwb_off + g)),
        pl.BlockSpec((1, n), lambda b, g: (0, cwc_off + g)),
    ]
    args = [proj, proj, proj, proj, acs_c, ldt_r, acs_r, dsk, norm_w,
            conv_w, conv_w, conv_w, conv_b, conv_b, conv_b]
    if h0 is not None:
        in_specs.append(pl.BlockSpec((None, 2, gw, n), lambda b, g: (b, 0, g, 0)))
        args.append(h0)
    if n_prev:
        in_specs.append(pl.BlockSpec((None, n_prev, 2, gw, n), lambda b, g: (b, 0, 0, g, 0)))
        args.append(prev_states)
    out_shape = [jax.ShapeDtypeStruct((nseq * seq_len, d_inner), BF16)]
    out_specs = [pl.BlockSpec((seq_len, gw), lambda b, g: (b, g))]
    if want_state:
        out_shape.append(jax.ShapeDtypeStruct((nseq, n_prev + 1, 2, d_inner, n), F32))
        out_specs.append(pl.BlockSpec((None, n_prev + 1, 2, gw, n), lambda b, g: (b, 0, 0, g, 0)))
    outs = pl.pallas_call(
        functools.partial(_ssd_kernel, nc, h0 is not None, want_state, n_prev),
        out_shape=out_shape,
        grid=(nseq, groups),
        in_specs=in_specs,
        out_specs=out_specs,
        scratch_shapes=[
            pltpu.VMEM((seq_len, gw), BF16),
            pltpu.VMEM((gw, seq_len), F32),
            pltpu.VMEM((seq_len, n), BF16),
            pltpu.VMEM((seq_len, n), BF16),
            pltpu.VMEM((seq_len, gw), F32),
            pltpu.VMEM((seq_len, gw), F32),
            pltpu.VMEM((2, gw, n), F32),
        ],
        compiler_params=_params("parallel", "parallel"),
        name="ssd_scan",
    )(*args)
    return outs if want_state else (outs[0], None)


def _ssd_mixer(x, norm_w_mix, mod5, layer, tok, geo, h0_sample, prev_states, j, w_in, conv_w, conv_b, dt_bias,
               a_log, d_skip, norm_w, w_out):
    heads = dt_bias.shape[1]
    groups = heads // SSD_HEADS_PER_GROUP
    d_inner = heads * SSD_HEAD_DIM
    hpg = SSD_HEADS_PER_GROUP
    t = x.shape[0]
    proj = _norm_mod_matmul(x, norm_w_mix, mod5, layer, tok, w_in, j)
    dt_off = d_inner + conv_w.shape[1]
    acs, acs_t, ldt_t = _ssd_dt(proj[:, dt_off:dt_off + 2 * heads], dt_bias, a_log)

    def rows_per_group(v):
        return v.reshape(2, groups, hpg, t).transpose(1, 0, 2, 3).reshape(groups, 2 * hpg, t)

    acs_c = acs.reshape(t, 2, groups, hpg).transpose(2, 0, 1, 3).reshape(groups, t, 2 * hpg)
    acs_r = rows_per_group(acs_t)
    ldt_r = rows_per_group(ldt_t)
    dsk = jnp.repeat(d_skip, SSD_HEAD_DIM).reshape(1, d_inner)
    nw = norm_w.reshape(1, d_inner)
    cb = conv_b.reshape(1, -1)
    bp, lp, bs, ls = geo
    y_p, states = _ssd_scan(proj, acs_c, ldt_r, acs_r, dsk, nw, conv_w, cb, 0, bp, lp, None, True, prev_states)
    h0 = h0_sample.reshape(bs, 2, d_inner, SSD_D_STATE)
    y_s, _ = _ssd_scan(proj, acs_c, ldt_r, acs_r, dsk, nw, conv_w, cb, bp * lp, bs, ls, h0, False)
    x = _matmul_residual(y_p, y_s, w_out, j, x, mod5, layer, tok)
    return x, states


def _rope(x, cos, sin):
    quarter = ATTN_HEAD_DIM // 4
    lane = lax.broadcasted_iota(jnp.int32, (1, ATTN_HEAD_DIM), 1)
    first = (lane % (2 * quarter)) < quarter
    swapped = jnp.where(first, pltpu.roll(x, ATTN_HEAD_DIM - quarter, 1), pltpu.roll(x, quarter, 1))
    return x * cos + swapped * sin


def _attn_kernel(nq, has_cache, use_rope, write_k, *refs):
    q_ref, k_ref, v_ref, qn_ref, kn_ref = refs[:5]
    pos = 5
    cos_ref = sin_ref = kc_ref = vc_ref = knew_ref = None
    if use_rope:
        cos_ref, sin_ref = refs[pos:pos + 2]
        pos += 2
    if has_cache:
        kc_ref, vc_ref = refs[pos:pos + 2]
        pos += 2
    o_ref = refs[pos]
    pos += 1
    if write_k:
        knew_ref = refs[pos]
        pos += 1
    kb_scr, vt_scr = refs[pos:pos + 2]
    pos += 2
    if has_cache:
        kcb_scr, vct_scr = refs[pos:pos + 2]
        kcb_scr[...] = kc_ref[...].astype(BF16)
        for c in range(vc_ref.shape[0] // CHUNK):
            rows = slice(c * CHUNK, (c + 1) * CHUNK)
            vct_scr[:, rows] = vc_ref[rows, :].T.astype(BF16)

    grp = q_ref.shape[1] // ATTN_HEAD_DIM
    scale = ATTN_HEAD_DIM ** -0.5

    def kprep(c, carry):
        r0 = pl.multiple_of(c * CHUNK, CHUNK)
        kn = _rms(k_ref[pl.ds(r0, CHUNK), :], kn_ref[...])
        if write_k:
            knew_ref[pl.ds(r0, CHUNK), :] = kn
        if use_rope:
            kn = _rope(kn, cos_ref[pl.ds(r0, CHUNK), :], sin_ref[pl.ds(r0, CHUNK), :])
        kb_scr[pl.ds(r0, CHUNK), :] = kn.astype(BF16)
        vt_scr[:, pl.ds(r0, CHUNK)] = v_ref[pl.ds(r0, CHUNK), :].T.astype(BF16)
        return carry

    lax.fori_loop(0, nq, kprep, 0)

    def qblock(c, carry):
        r0 = pl.multiple_of(c * CHUNK, CHUNK)
        q_t = []
        for g in range(grp):
            cols = slice(ATTN_HEAD_DIM * g, ATTN_HEAD_DIM * (g + 1))
            qn = _rms(q_ref[pl.ds(r0, CHUNK), cols], qn_ref[...])
            if use_rope:
                qn = _rope(qn, cos_ref[pl.ds(r0, CHUNK), :], sin_ref[pl.ds(r0, CHUNK), :])
            q_t.append((qn * (scale * LOG2_E)).T.astype(BF16))
        q_t = jnp.concatenate(q_t, axis=1)
        s = jnp.dot(kb_scr[...], q_t, preferred_element_type=F32)
        m = jnp.max(s, axis=0, keepdims=True)
        if has_cache:
            sc = jnp.dot(kcb_scr[...], q_t, preferred_element_type=F32)
            m = jnp.maximum(m, jnp.max(sc, axis=0, keepdims=True))
            ec = jnp.exp2(sc - m)
        e = jnp.exp2(s - m)
        den = jnp.sum(e, axis=0, keepdims=True)
        if has_cache:
            den = den + jnp.sum(ec, axis=0, keepdims=True)
        o_t = jnp.dot(vt_scr[...], e.astype(BF16), preferred_element_type=F32)
        if has_cache:
            o_t += jnp.dot(vct_scr[...], ec.astype(BF16), preferred_element_type=F32)
        o_t = o_t * (1.0 / den)
        for g in range(grp):
            cols = slice(ATTN_HEAD_DIM * g, ATTN_HEAD_DIM * (g + 1))
            o_ref[pl.ds(r0, CHUNK), cols] = o_t[:, cols].T.astype(o_ref.dtype)
        return carry

    lax.fori_loop(0, nq, qblock, 0, unroll=2)


def _attention(qkv, q_norm, k_norm, heads, kv_heads, row0, nseq, seq_len, rope, cache, write_k):
    dh = ATTN_HEAD_DIM
    grp = heads // kv_heads
    rb0 = row0 // seq_len
    nq = seq_len // CHUNK
    in_specs = [
        pl.BlockSpec((seq_len, grp * dh), lambda b, h: (rb0 + b, h)),
        pl.BlockSpec((seq_len, dh), lambda b, h: (rb0 + b, heads + h)),
        pl.BlockSpec((seq_len, dh), lambda b, h: (rb0 + b, heads + kv_heads + h)),
        pl.BlockSpec((1, dh), lambda b, h: (0, 0)),
        pl.BlockSpec((1, dh), lambda b, h: (0, 0)),
    ]
    args = [qkv, qkv, qkv, q_norm.reshape(1, dh), k_norm.reshape(1, dh)]
    scratch = [pltpu.VMEM((seq_len, dh), BF16), pltpu.VMEM((dh, seq_len), BF16)]
    if rope is not None:
        in_specs += [pl.BlockSpec((seq_len, dh), lambda b, h: (0, 0))] * 2
        args += list(rope)
    if cache is not None:
        past = cache[0].shape[1]
        in_specs += [pl.BlockSpec((None, past, dh), lambda b, h: (b, 0, h))] * 2
        args += list(cache)
        scratch += [pltpu.VMEM((past, dh), BF16), pltpu.VMEM((dh, past), BF16)]
    out_shape = [jax.ShapeDtypeStruct((nseq * seq_len, heads * dh), BF16)]
    out_specs = [pl.BlockSpec((seq_len, grp * dh), lambda b, h: (b, h))]
    if write_k:
        out_shape.append(jax.ShapeDtypeStruct((nseq * seq_len, kv_heads * dh), F32))
        out_specs.append(pl.BlockSpec((seq_len, dh), lambda b, h: (b, h)))
    outs = pl.pallas_call(
        functools.partial(_attn_kernel, nq, cache is not None, rope is not None, write_k),
        out_shape=out_shape,
        grid=(nseq, kv_heads),
        in_specs=in_specs,
        out_specs=out_specs,
        scratch_shapes=scratch,
        compiler_params=_params("parallel", "parallel"),
        name="attention",
    )(*args)
    return outs if write_k else (outs[0], None)


def _rope_tables(seq_len):
    quarter = ATTN_HEAD_DIM // 4
    t = jnp.arange(seq_len)
    lane = jnp.arange(ATTN_HEAD_DIM)
    pos = jnp.where(lane[None, :] < 2 * quarter, (t // GRID_W)[:, None], (t % GRID_W)[:, None]).astype(F32)
    inv_freq = ROPE_THETA ** (-jnp.arange(quarter, dtype=F32) / quarter)
    ang = pos * inv_freq[lane % quarter][None, :]
    sign = jnp.where((lane % (2 * quarter)) < quarter, -1.0, 1.0).astype(F32)
    return jnp.cos(ang), jnp.sin(ang) * sign[None, :]


def _attn_mixer(x, norm_w_mix, mod5, layer, tok, geo, cache_k, cache_v, j, w_qkv, q_norm, k_norm, w_out):
    bp, lp, bs, ls = geo
    dh = ATTN_HEAD_DIM
    kv_heads = cache_k.shape[2]
    heads = w_out.shape[1] // dh
    qkv = _norm_mod_matmul(x, norm_w_mix, mod5, layer, tok, w_qkv, j)
    o_p, k_new = _attention(qkv, q_norm, k_norm, heads, kv_heads, 0, bp, lp, None, None, True)
    past = cache_k.shape[1]
    cache = (cache_k.reshape(bs, past, kv_heads * dh), cache_v.reshape(bs, past, kv_heads * dh))
    o_s, _ = _attention(qkv, q_norm, k_norm, heads, kv_heads, bp * lp, bs, ls, _rope_tables(ls), cache, False)
    x = _matmul_residual(o_p, o_s, w_out, j, x, mod5, layer, tok)
    v_new = qkv[:bp * lp, (heads + kv_heads) * dh:]
    return x, k_new.reshape(bp, lp, kv_heads, dh), v_new.reshape(bp, lp, kv_heads, dh)


def _mlstm_gate_kernel(heads, i_ref, f_ref, bi_ref, bf_ref, b_ref, r_ref, p_ref):
    forward = lax.broadcasted_iota(jnp.int32, (1, i_ref.shape[1]), 1) < heads
    for c in range(i_ref.shape[0] // CHUNK):
        rows = slice(c * CHUNK, (c + 1) * CHUNK)
        log_f = _log_sigmoid(f_ref[rows, :] + bf_ref[...])
        b = jnp.where(forward, _cumsum_chunk(log_f, False), _cumsum_chunk(log_f, True))
        r = i_ref[rows, :] + bi_ref[...] - b
        b_ref[rows, :] = b
        r_ref[rows, :] = r
        p_ref[rows, :] = jnp.where(forward, _cummax_chunk(r, False), _cummax_chunk(r, True))


def _mlstm_gates(i_raw, f_raw, b_i, b_f):
    t, cols = i_raw.shape
    tm = _pick_tile(t, (1024, 512, 256, 128))
    return pl.pallas_call(
        functools.partial(_mlstm_gate_kernel, cols // 2),
        out_shape=[jax.ShapeDtypeStruct((t, cols), F32)] * 3,
        grid=(t // tm,),
        in_specs=[
            pl.BlockSpec((tm, cols), lambda i: (i, 0)),
            pl.BlockSpec((tm, cols), lambda i: (i, 0)),
            pl.BlockSpec((1, cols), lambda i: (0, 0)),
            pl.BlockSpec((1, cols), lambda i: (0, 0)),
        ],
        out_specs=[pl.BlockSpec((tm, cols), lambda i: (i, 0))] * 3,
        compiler_params=_params("parallel"),
        name="mlstm_gates",
    )(i_raw, f_raw, b_i, b_f)


def _mlstm_kernel(nc, has_s0, has_st, *refs):
    q_ref, k_ref, v_ref, og_ref, cp_ref, rr_ref, hn_ref = refs[:7]
    pos = 7
    c0_ref = n0_ref = m0_ref = ct_ref = nt_ref = mt_ref = None
    if has_s0:
        c0_ref, n0_ref, m0_ref = refs[pos:pos + 3]
        pos += 3
    h_ref = refs[pos]
    pos += 1
    if has_st:
        ct_ref, nt_ref, mt_ref = refs[pos:pos + 3]
        pos += 3
    hf_scr, hb_scr, c_scr, n_scr, m_scr = refs[pos:]

    dk = q_ref.shape[1]
    dv = v_ref.shape[1]
    qscale = dk ** -0.5
    tile = (CHUNK, CHUNK)
    ti = lax.broadcasted_iota(jnp.int32, tile, 0)
    si = lax.broadcasted_iota(jnp.int32, tile, 1)

    def wide(x):
        return jnp.concatenate([x] * (dv // CHUNK), axis=1)

    def chunk_step(c, d):
        r0 = pl.multiple_of(c * CHUNK, CHUNK)
        q = q_ref[pl.ds(r0, CHUNK), :] * qscale
        k = k_ref[pl.ds(r0, CHUNK), :]
        qb = q.astype(BF16)
        kb = k.astype(BF16)
        vb = v_ref[pl.ds(r0, CHUNK), :].astype(BF16)
        cp = cp_ref[pl.ds(r0, CHUNK), :]
        b_t = jnp.broadcast_to(cp[:, 3 * d:3 * d + 1], tile)
        r_t = jnp.broadcast_to(cp[:, 3 * d + 1:3 * d + 2], tile)
        p_t = jnp.broadcast_to(cp[:, 3 * d + 2:3 * d + 3], tile)
        r_row = rr_ref[d:d + 1, pl.ds(r0, CHUNK)]
        m_row = m_scr[d]
        u_t = jnp.maximum(m_row, p_t)
        mask = (ti >= si) if d == 0 else (ti <= si)
        s = lax.dot_general(qb, kb, NT_DIMS, preferred_element_type=F32)
        s = s * jnp.exp(jnp.where(mask, r_row - u_t, -jnp.inf))
        inter = jnp.exp(m_row - u_t)
        cst = c_scr[d]
        nst = n_scr[d]
        num = jnp.dot(s.astype(BF16), vb, preferred_element_type=F32)
        num += wide(inter) * jnp.dot(qb, cst.astype(BF16), preferred_element_type=F32)
        den = jnp.sum(s + inter * (q * nst), axis=1, keepdims=True)
        rcp = 1.0 / jnp.maximum(jnp.abs(den), jnp.exp(-(b_t + u_t)))
        h = num * wide(rcp)
        last = CHUNK - 1 if d == 0 else 0
        u_last = u_t[last:last + 1, :]
        wk = jnp.exp(r_t - u_last) * k
        carry_decay = jnp.exp(m_row - u_last)
        c_scr[d] = wide(carry_decay) * cst + jnp.dot(wk.T.astype(BF16), vb, preferred_element_type=F32)
        n_scr[d] = carry_decay * nst + jnp.sum(wk, axis=0, keepdims=True)
        m_scr[d] = b_t[last:last + 1, :] + u_last
        (hf_scr if d == 0 else hb_scr)[pl.ds(r0, CHUNK), :] = h

    for d in range(2):
        if has_s0:
            c_scr[d] = c0_ref[d]
            n_scr[d] = n0_ref[d]
            m_scr[d] = m0_ref[d]
        else:
            c_scr[d] = jnp.zeros(c_scr.shape[1:], F32)
            n_scr[d] = jnp.zeros(n_scr.shape[1:], F32)
            m_scr[d] = jnp.zeros(m_scr.shape[1:], F32)

    def scan_body(i, carry):
        chunk_step(i, 0)
        chunk_step(nc - 1 - i, 1)
        return carry

    lax.fori_loop(0, nc, scan_body, 0, unroll=min(nc, 4))

    def finish(c, carry):
        r0 = pl.multiple_of(c * CHUNK, CHUNK)
        hs = _rms(hf_scr[pl.ds(r0, CHUNK), :] + hb_scr[pl.ds(r0, CHUNK), :], hn_ref[...])
        h_ref[pl.ds(r0, CHUNK), :] = (hs * _sigmoid(og_ref[pl.ds(r0, CHUNK), :])).astype(h_ref.dtype)
        return carry

    lax.fori_loop(0, nc, finish, 0, unroll=2)
    if has_st:
        for d in range(2):
            ct_ref[d] = c_scr[d]
            nt_ref[d] = n_scr[d]
            mt_ref[d] = m_scr[d]


def _mlstm_scan(proj, gate_cols, gate_rows, head_norm, heads, dk, dv, row0, nseq, seq_len, state, want_state):
    assert dv == 2 * dk and dk == CHUNK
    nc = seq_len // CHUNK
    rb0 = row0 // seq_len
    in_specs = [
        pl.BlockSpec((seq_len, dk), lambda b, h: (rb0 + b, h)),
        pl.BlockSpec((seq_len, dk), lambda b, h: (rb0 + b, heads + h)),
        pl.BlockSpec((seq_len, dv), lambda b, h: (rb0 + b, heads + h)),
        pl.BlockSpec((seq_len, dv), lambda b, h: (rb0 + b, 2 * heads + h)),
        pl.BlockSpec((None, seq_len, 6), lambda b, h: (h, rb0 + b, 0)),
        pl.BlockSpec((None, 2, seq_len), lambda b, h: (h, 0, rb0 + b)),
        pl.BlockSpec((None, 1, dv), lambda b, h: (h, 0, 0)),
    ]
    args = [proj, proj, proj, proj, gate_cols, gate_rows, head_norm]
    state_specs = [
        pl.BlockSpec((None, 2, None, dk, dv), lambda b, h: (b, 0, h, 0, 0)),
        pl.BlockSpec((None, 2, None, 1, dk), lambda b, h: (b, 0, h, 0, 0)),
        pl.BlockSpec((None, 2, None, 1, LANES), lambda b, h: (b, 0, h, 0, 0)),
    ]
    if state is not None:
        in_specs += state_specs
        args += list(state)
    out_shape = [jax.ShapeDtypeStruct((nseq * seq_len, heads * dv), BF16)]
    out_specs = [pl.BlockSpec((seq_len, dv), lambda b, h: (b, h))]
    if want_state:
        out_shape += [jax.ShapeDtypeStruct((nseq, 2, heads, dk, dv), F32),
                      jax.ShapeDtypeStruct((nseq, 2, heads, 1, dk), F32),
                      jax.ShapeDtypeStruct((nseq, 2, heads, 1, LANES), F32)]
        out_specs += state_specs
    outs = pl.pallas_call(
        functools.partial(_mlstm_kernel, nc, state is not None, want_state),
        out_shape=out_shape,
        grid=(nseq, heads),
        in_specs=in_specs,
        out_specs=out_specs,
        scratch_shapes=[
            pltpu.VMEM((seq_len, dv), F32),
            pltpu.VMEM((seq_len, dv), F32),
            pltpu.VMEM((2, dk, dv), F32),
            pltpu.VMEM((2, 1, dk), F32),
            pltpu.VMEM((2, 1, LANES), F32),
        ],
        compiler_params=_params("parallel", "parallel"),
        name="mlstm_scan",
    )(*args)
    return outs


def _mlstm_mixer(x, norm_w_mix, mod5, layer, tok, geo, c0, n0, m0, j, w_in, b_gates, head_norm, w_out):
    bp, lp, bs, ls = geo
    heads, dk, dv = c0.shape[2], c0.shape[3], c0.shape[4]
    t = x.shape[0]
    proj = _norm_mod_matmul(x, norm_w_mix, mod5, layer, tok, w_in, j)
    g_off = 2 * heads * dk + 2 * heads * dv
    graw = proj[:, g_off:g_off + 4 * heads].reshape(t, 2, 2, heads)
    b, r, p = _mlstm_gates(graw[:, :, 0, :].reshape(t, 2 * heads), graw[:, :, 1, :].reshape(t, 2 * heads),
                           b_gates[:, 0, :].reshape(1, 2 * heads), b_gates[:, 1, :].reshape(1, 2 * heads))
    gate_cols = jnp.stack([b, r, p], axis=-1).reshape(t, 2, heads, 3).transpose(2, 0, 1, 3).reshape(heads, t, 6)
    gate_rows = r.reshape(t, 2, heads).transpose(2, 1, 0)
    hn = head_norm.reshape(heads, 1, dv)
    outs_p = _mlstm_scan(proj, gate_cols, gate_rows, hn, heads, dk, dv, 0, bp, lp, None, True)
    h_p, c_t, n_t, m_t = outs_p
    state = (c0, n0.reshape(bs, 2, heads, 1, dk),
             jnp.broadcast_to(m0[..., None, None], (bs, 2, heads, 1, LANES)))
    (h_s,) = _mlstm_scan(proj, gate_cols, gate_rows, hn, heads, dk, dv, bp * lp, bs, ls, state, False)
    x = _matmul_residual(h_p, h_s, w_out, j, x, mod5, layer, tok)
    return x, c_t, n_t.reshape(bp, 2, heads, dk), m_t[:, :, :, 0, 0]


def kernel(x_prompt, x_sample, c, state_ssd, cache_attn_k, cache_attn_v, state_mlstm_C, state_mlstm_n, state_mlstm_m, c_ctx, ada_w, ada_b, norm_mix_w, norm_ffn_w, ffn_w_gate, ffn_w_up, ffn_w_down, ssd_w_in, ssd_conv_w, ssd_conv_b, ssd_dt_bias, ssd_a_log, ssd_d, ssd_norm_w, ssd_w_out, attn_w_qkv, attn_q_norm, attn_k_norm, attn_w_out, mlstm_w_in, mlstm_b_gates, mlstm_head_norm, mlstm_w_out):
    bp, lp, d = x_prompt.shape
    bs, ls, _ = x_sample.shape
    depth = ada_w.shape[0]
    geo = (bp, lp, bs, ls)
    tok = _Tokens(bp * lp, ls)
    x = jnp.concatenate([x_prompt.reshape(bp * lp, d), x_sample.reshape(bs * ls, d)], axis=0)

    mod_rows = -(-(1 + bs) // SUBLANES) * SUBLANES
    cc = jnp.concatenate([c_ctx[None, :], c, jnp.zeros((mod_rows - 1 - bs, d), F32)], axis=0)
    mod5 = _modulation(cc, ada_w, ada_b).reshape(depth, mod_rows, 6, 1, d)

    w_gate, w_up, w_down = _to_bf16(ffn_w_gate), _to_bf16(ffn_w_up), _to_bf16(ffn_w_down)
    ssd_in, ssd_out = _to_bf16(ssd_w_in, 6 * MXU_COLS), _to_bf16(ssd_w_out)
    attn_qkv, attn_out = _to_bf16(attn_w_qkv), _to_bf16(attn_w_out)
    mlstm_in = _to_bf16(mlstm_w_in, 5 * MXU_COLS)
    mlstm_out = _to_bf16(mlstm_w_out)
    ssd_states = None
    new_k, new_v, new_c, new_n, new_m = [], [], [], [], []
    for l in range(depth):
        kind, j = l % 3, l // 3
        if kind == 0:
            x, ssd_states = _ssd_mixer(x, norm_mix_w[l], mod5, l, tok, geo, state_ssd[:, j], ssd_states, j, ssd_in,
                                       ssd_conv_w[j], ssd_conv_b[j], ssd_dt_bias[j], ssd_a_log[j], ssd_d[j],
                                       ssd_norm_w[j], ssd_out)
        elif kind == 1:
            x, k_new, v_new = _attn_mixer(x, norm_mix_w[l], mod5, l, tok, geo, cache_attn_k[:, j], cache_attn_v[:, j],
                                          j, attn_qkv, attn_q_norm[j], attn_k_norm[j], attn_out)
            new_k.append(k_new)
            new_v.append(v_new)
        else:
            x, c_t, n_t, m_t = _mlstm_mixer(x, norm_mix_w[l], mod5, l, tok, geo, state_mlstm_C[:, j],
                                            state_mlstm_n[:, j], state_mlstm_m[:, j], j, mlstm_in,
                                            mlstm_b_gates[j], mlstm_head_norm[j], mlstm_out)
            new_c.append(c_t)
            new_n.append(n_t)
            new_m.append(m_t)
        x = _ffn(x, norm_ffn_w[l], mod5, l, tok, w_gate, w_up, w_down, split_out=(l == depth - 1))

    y_prompt = x[0].reshape(bp, lp, d)
    y_sample = x[1].reshape(bs, ls, d)
    ssd_states = ssd_states.reshape((bp,) + state_ssd.shape[1:])
    return (y_prompt, y_sample, ssd_states, jnp.stack(new_k, axis=1), jnp.stack(new_v, axis=1),
            jnp.stack(new_c, axis=1), jnp.stack(new_n, axis=1), jnp.stack(new_m, axis=1))
```

```python
import functools

import jax
import jax.numpy as jnp
from jax import lax
from jax.experimental import pallas as pl
from jax.experimental.pallas import tpu as pltpu

F32 = jnp.float32
BF16 = jnp.bfloat16

EPS = 1e-6
CHUNK = 128
GRID_W = 64
ROPE_THETA = 10000.0
SSD_HEAD_DIM = 64
SSD_D_STATE = 128
SSD_CONV_W = 3
SSD_HEADS_PER_GROUP = 8
ATTN_HEAD_DIM = 128
LANES = 128
SUBLANES = 8
MXU_COLS = 256
NORM_ROWS = 16
VMEM_LIMIT_BYTES = 56 * 1024 * 1024

NT_DIMS = (((1,), (1,)), ((), ()))
LOG2_E = 1.4426950408889634


def _sigmoid(x):
    return 0.5 + 0.5 * jnp.tanh(0.5 * x)


def _silu(x):
    h = 0.5 * x
    return h + h * jnp.tanh(h)


def _softplus(x):
    return jnp.maximum(x, 0.0) + jnp.log1p(jnp.exp(-jnp.abs(x)))


def _log_sigmoid(x):
    return -_softplus(-x)


def _rms(x, w):
    return x * lax.rsqrt(jnp.mean(x * x, axis=-1, keepdims=True) + EPS) * w


def _params(*sem):
    return pltpu.CompilerParams(dimension_semantics=sem, vmem_limit_bytes=VMEM_LIMIT_BYTES)


def _pick_tile(n, candidates):
    for c in candidates:
        if n % c == 0:
            return c
    return n


def _scan_chunk(a, reverse, combine, identity):
    idx = lax.broadcasted_iota(jnp.int32, (CHUNK, 1), 0)
    sh = 1
    while sh < CHUNK:
        if reverse:
            a = combine(a, jnp.where(idx < CHUNK - sh, pltpu.roll(a, CHUNK - sh, 0), identity))
        else:
            a = combine(a, jnp.where(idx >= sh, pltpu.roll(a, sh, 0), identity))
        sh *= 2
    return a


def _cumsum_chunk(a, reverse):
    return _scan_chunk(a, reverse, jnp.add, 0.0)


def _cummax_chunk(a, reverse):
    return _scan_chunk(a, reverse, jnp.maximum, -jnp.inf)


def _cast_kernel(w_ref, o_ref):
    n = w_ref.shape[1]
    o_ref[:, :n] = w_ref[...].astype(BF16)
    if o_ref.shape[1] > n:
        o_ref[:, n:] = jnp.zeros((o_ref.shape[0], o_ref.shape[1] - n), BF16)


def _to_bf16(w, col_multiple=1):
    shape = w.shape
    w2 = w.reshape(-1, shape[-1])
    rows, n = w2.shape
    n_out = -(-n // col_multiple) * col_multiple
    if n_out != n and n % LANES:
        return jnp.pad(_to_bf16(w), [(0, 0)] * (len(shape) - 1) + [(0, n_out - n)])
    tr = _pick_tile(rows, (256, 128, 64, 32, 16))
    out = pl.pallas_call(
        _cast_kernel,
        out_shape=jax.ShapeDtypeStruct((rows, n_out), BF16),
        grid=(rows // tr,),
        in_specs=[pl.BlockSpec((tr, n), lambda i: (i, 0))],
        out_specs=pl.BlockSpec((tr, n_out), lambda i: (i, 0)),
        compiler_params=_params("parallel"),
        name="cast_bf16",
    )(w2)
    return out.reshape(shape[:-1] + (n_out,))


def _mod_kernel(c_ref, w_ref, b_ref, o_ref):
    s = _silu(c_ref[...]).astype(BF16)
    o_ref[...] = jnp.dot(s, w_ref[...].astype(BF16), preferred_element_type=F32) + b_ref[...]


def _modulation(cc, ada_w, ada_b):
    depth, d, n = ada_w.shape
    rows = cc.shape[0]
    tn = _pick_tile(n, (1024, 512, 256, 128))
    return pl.pallas_call(
        _mod_kernel,
        out_shape=jax.ShapeDtypeStruct((depth, rows, n), F32),
        grid=(depth, n // tn),
        in_specs=[
            pl.BlockSpec((rows, d), lambda l, j: (0, 0)),
            pl.BlockSpec((None, d, tn), lambda l, j: (l, 0, j)),
            pl.BlockSpec((None, 1, tn), lambda l, j: (l, 0, j)),
        ],
        out_specs=pl.BlockSpec((None, rows, tn), lambda l, j: (l, 0, j)),
        compiler_params=_params("parallel", "parallel"),
        name="adaln_mod",
    )(cc, ada_w, ada_b.reshape(depth, 1, n))


class _Tokens:
    def __init__(self, n_prompt_tok, dec_seq):
        self.n_prompt_tok = n_prompt_tok
        self.dec_seq = dec_seq

    def row(self, start):
        return jnp.where(start < self.n_prompt_tok, 0, 1 + (start - self.n_prompt_tok) // self.dec_seq)

    def tile(self, candidates):
        for c in candidates:
            if self.n_prompt_tok % c == 0 and self.dec_seq % c == 0:
                return c
        raise ValueError("no token tile fits the sequence layout")


def _mod_spec(tok, tm, layer, k, d, grid_pos):
    def index_map(*ids):
        return (layer, tok.row(ids[grid_pos] * tm), k, 0, 0)
    return pl.BlockSpec((None, None, None, 1, d), index_map)


def _norm_mod(u_ref, x_ref, nw_ref, sh_ref, sc_ref):
    gain = nw_ref[...] * (1.0 + sc_ref[...])
    shift = sh_ref[...]

    def body(r, carry):
        r0 = pl.multiple_of(r * NORM_ROWS, NORM_ROWS)
        x = x_ref[pl.ds(r0, NORM_ROWS), :]
        inv = lax.rsqrt(jnp.mean(x * x, axis=-1, keepdims=True) + EPS)
        u_ref[pl.ds(r0, NORM_ROWS), :] = (x * inv * gain + shift).astype(BF16)
        return carry

    lax.fori_loop(0, x_ref.shape[0] // NORM_ROWS, body, 0, unroll=8)


def _nmm_kernel(x_ref, nw_ref, sh_ref, sc_ref, w_ref, o_ref, u_ref):
    @pl.when(pl.program_id(1) == 0)
    def _():
        _norm_mod(u_ref, x_ref, nw_ref, sh_ref, sc_ref)

    o_ref[...] = jnp.dot(u_ref[...], w_ref[...], preferred_element_type=F32)


def _norm_mod_matmul(x, norm_w, mod5, layer, tok, w, widx):
    t, d = x.shape
    n = w.shape[2]
    tm = tok.tile((1024, 512, 256, 128))
    tn = _pick_tile(n, (1536, 1280, 1024, 768, 512, 256, 128))
    return pl.pallas_call(
        _nmm_kernel,
        out_shape=jax.ShapeDtypeStruct((t, n), F32),
        grid=(t // tm, n // tn),
        in_specs=[
            pl.BlockSpec((tm, d), lambda i, j: (i, 0)),
            pl.BlockSpec((1, d), lambda i, j: (0, 0)),
            _mod_spec(tok, tm, layer, 0, d, 0),
            _mod_spec(tok, tm, layer, 1, d, 0),
            pl.BlockSpec((None, d, tn), lambda i, j: (widx, 0, j)),
        ],
        out_specs=pl.BlockSpec((tm, tn), lambda i, j: (i, j)),
        scratch_shapes=[pltpu.VMEM((tm, d), BF16)],
        compiler_params=_params("parallel", "arbitrary"),
        name="norm_mod_matmul",
    )(x, norm_w.reshape(1, d), mod5, mod5, w)


def _mmres_kernel(prompt_tiles, yp_ref, ys_ref, w_ref, x_ref, g_ref, o_ref):
    def run(y_ref):
        o_ref[...] = x_ref[...] + g_ref[...] * jnp.dot(y_ref[...], w_ref[...], preferred_element_type=F32)

    is_prompt = pl.program_id(1) < prompt_tiles
    pl.when(is_prompt)(lambda: run(yp_ref))
    pl.when(jnp.logical_not(is_prompt))(lambda: run(ys_ref))


def _matmul_residual(y_prompt, y_sample, w, widx, x, mod5, layer, tok):
    kdim = y_prompt.shape[1]
    t, d = x.shape
    tm = tok.tile((512, 256, 128))
    tn = _pick_tile(d, (1024, 512, 256, 128))
    nj = d // tn
    prompt_tiles = tok.n_prompt_tok // tm

    def gate_map(j, i):
        return (layer, tok.row(i * tm), 2, 0, j)

    return pl.pallas_call(
        functools.partial(_mmres_kernel, prompt_tiles),
        out_shape=jax.ShapeDtypeStruct((t, d), F32),
        grid=(nj, t // tm),
        in_specs=[
            pl.BlockSpec((tm, kdim), lambda j, i: (jnp.minimum(i, prompt_tiles - 1), 0)),
            pl.BlockSpec((tm, kdim), lambda j, i: (jnp.maximum(i - prompt_tiles, 0), 0)),
            pl.BlockSpec((None, kdim, tn), lambda j, i: (widx, 0, j)),
            pl.BlockSpec((tm, tn), lambda j, i: (i, j)),
            pl.BlockSpec((None, None, None, 1, tn), gate_map),
        ],
        out_specs=pl.BlockSpec((tm, tn), lambda j, i: (i, j)),
        compiler_params=_params("parallel", "parallel"),
        name="matmul_residual",
    )(y_prompt, y_sample, w, x, mod5)


def _ffn_kernel(prompt_tiles, x_ref, nw_ref, sh_ref, sc_ref, g_ref, wg_ref, wu_ref, wd_ref, *refs):
    u_ref, acc_ref = refs[-2:]
    j = pl.program_id(1)

    @pl.when(j == 0)
    def _():
        _norm_mod(u_ref, x_ref, nw_ref, sh_ref, sc_ref)
        acc_ref[...] = jnp.zeros_like(acc_ref)

    u = u_ref[...]
    a = jnp.dot(u, wg_ref[...], preferred_element_type=F32)
    b = jnp.dot(u, wu_ref[...], preferred_element_type=F32)
    h = (_silu(a) * b).astype(BF16)
    acc_ref[...] += jnp.dot(h, wd_ref[...], preferred_element_type=F32)

    last = j == pl.num_programs(1) - 1
    if prompt_tiles is None:
        (o_ref,) = refs[:-2]

        @pl.when(last)
        def _():
            o_ref[...] = x_ref[...] + g_ref[...] * acc_ref[...]
    else:
        op_ref, os_ref = refs[:-2]
        is_prompt = pl.program_id(0) < prompt_tiles

        @pl.when(jnp.logical_and(last, is_prompt))
        def _():
            op_ref[...] = x_ref[...] + g_ref[...] * acc_ref[...]

        @pl.when(jnp.logical_and(last, jnp.logical_not(is_prompt)))
        def _():
            os_ref[...] = x_ref[...] + g_ref[...] * acc_ref[...]


def _ffn(x, norm_w, mod5, layer, tok, wg, wu, wd, split_out=False):
    t, d = x.shape
    hid = wg.shape[2]
    tm = tok.tile((512, 256, 128))
    th = _pick_tile(hid, (512, 256, 128))
    if split_out:
        prompt_tiles = tok.n_prompt_tok // tm
        out_shape = [jax.ShapeDtypeStruct((tok.n_prompt_tok, d), F32),
                     jax.ShapeDtypeStruct((t - tok.n_prompt_tok, d), F32)]
        out_specs = [pl.BlockSpec((tm, d), lambda i, j: (jnp.minimum(i, prompt_tiles - 1), 0)),
                     pl.BlockSpec((tm, d), lambda i, j: (jnp.maximum(i - prompt_tiles, 0), 0))]
    else:
        prompt_tiles = None
        out_shape = jax.ShapeDtypeStruct((t, d), F32)
        out_specs = pl.BlockSpec((tm, d), lambda i, j: (i, 0))
    return pl.pallas_call(
        functools.partial(_ffn_kernel, prompt_tiles),
        out_shape=out_shape,
        grid=(t // tm, hid // th),
        in_specs=[
            pl.BlockSpec((tm, d), lambda i, j: (i, 0)),
            pl.BlockSpec((1, d), lambda i, j: (0, 0)),
            _mod_spec(tok, tm, layer, 3, d, 0),
            _mod_spec(tok, tm, layer, 4, d, 0),
            _mod_spec(tok, tm, layer, 5, d, 0),
            pl.BlockSpec((None, d, th), lambda i, j: (layer, 0, j)),
            pl.BlockSpec((None, d, th), lambda i, j: (layer, 0, j)),
            pl.BlockSpec((None, th, d), lambda i, j: (layer, j, 0)),
        ],
        out_specs=out_specs,
        scratch_shapes=[pltpu.VMEM((tm, d), BF16), pltpu.VMEM((tm, d), F32)],
        compiler_params=_params("arbitrary" if split_out else "parallel", "arbitrary"),
        name="ffn",
    )(x, norm_w.reshape(1, d), mod5, mod5, mod5, wg, wu, wd)


def _ssd_dt_kernel(heads, raw_ref, bias_ref, alog_ref, acs_ref, acst_ref, ldtt_ref):
    forward = lax.broadcasted_iota(jnp.int32, (1, raw_ref.shape[1]), 1) < heads
    neg_a = -jnp.exp(alog_ref[...])
    for c in range(raw_ref.shape[0] // CHUNK):
        rows = slice(c * CHUNK, (c + 1) * CHUNK)
        dt = _softplus(raw_ref[rows, :] + bias_ref[...])
        a = dt * neg_a
        acs2 = jnp.where(forward, _cumsum_chunk(a, False), _cumsum_chunk(a, True)) * LOG2_E
        acs_ref[rows, :] = acs2
        acst_ref[:, rows] = acs2.T
        ldtt_ref[:, rows] = (jnp.log(dt) * LOG2_E - acs2).T


def _ssd_dt(dt_raw, dt_bias, a_log):
    t, cols = dt_raw.shape
    tm = _pick_tile(t, (1024, 512, 256, 128))
    return pl.pallas_call(
        functools.partial(_ssd_dt_kernel, cols // 2),
        out_shape=[jax.ShapeDtypeStruct((t, cols), F32), jax.ShapeDtypeStruct((cols, t), F32),
                   jax.ShapeDtypeStruct((cols, t), F32)],
        grid=(t // tm,),
        in_specs=[
            pl.BlockSpec((tm, cols), lambda i: (i, 0)),
            pl.BlockSpec((1, cols), lambda i: (0, 0)),
            pl.BlockSpec((1, cols), lambda i: (0, 0)),
        ],
        out_specs=[pl.BlockSpec((tm, cols), lambda i: (i, 0)), pl.BlockSpec((cols, tm), lambda i: (0, i)),
                   pl.BlockSpec((cols, tm), lambda i: (0, i))],
        compiler_params=_params("parallel"),
        name="ssd_dt",
    )(dt_raw, dt_bias.reshape(1, cols), a_log.reshape(1, cols))


def _ssd_kernel(nc, has_h0, has_ht, n_prev, *refs):
    (z_ref, x_ref, b_ref, c_ref, acsc_ref, ldtr_ref, acsr_ref, dsk_ref, nw_ref,
     cwx_ref, cwb_ref, cwc_ref, cbx_ref, cbb_ref, cbc_ref) = refs[:15]
    pos = 15
    h0_ref = None
    ht_ref = None
    prev_ref = None
    if has_h0:
        h0_ref = refs[pos]
        pos += 1
    if n_prev:
        prev_ref = refs[pos]
        pos += 1
    y_ref = refs[pos]
    pos += 1
    if has_ht:
        ht_ref = refs[pos]
        pos += 1
    xb_scr, xt_scr, bs_scr, cs_scr, yf_scr, yb_scr, h_scr = refs[pos:]

    seq_len = nc * CHUNK
    pairs = x_ref.shape[1] // LANES
    row = lax.broadcasted_iota(jnp.int32, (CHUNK, 1), 0)
    upper = row < SSD_HEAD_DIM
    lo = lax.broadcasted_iota(jnp.int32, (1, LANES), 1) < SSD_HEAD_DIM
    ti = lax.broadcasted_iota(jnp.int32, (CHUNK, CHUNK), 0)
    si = lax.broadcasted_iota(jnp.int32, (CHUNK, CHUNK), 1)

    def conv_silu(src, w_ref, bias_ref, c, r0):
        rp = pl.multiple_of(jnp.maximum(r0 - SUBLANES, 0), SUBLANES)
        rn = pl.multiple_of(jnp.minimum(r0 + CHUNK, seq_len - SUBLANES), SUBLANES)
        cur = src[pl.ds(r0, CHUNK), :]
        prow = jnp.where(c > 0, src[pl.ds(rp, SUBLANES), :][SUBLANES - 1:SUBLANES, :], 0.0)
        nrow = jnp.where(c < nc - 1, src[pl.ds(rn, SUBLANES), :][0:1, :], 0.0)
        xm1 = jnp.where(row == 0, prow, pltpu.roll(cur, 1, 0))
        xp1 = jnp.where(row == CHUNK - 1, nrow, pltpu.roll(cur, CHUNK - 1, 0))
        w = w_ref[...]
        return _silu(w[0:1, :] * xm1 + w[1:2, :] * cur + w[2:3, :] * xp1 + bias_ref[...])

    def conv_chunk(c, carry):
        r0 = pl.multiple_of(c * CHUNK, CHUNK)
        xs = conv_silu(x_ref, cwx_ref, cbx_ref, c, r0)
        xb_scr[pl.ds(r0, CHUNK), :] = xs.astype(BF16)
        yf_scr[pl.ds(r0, CHUNK), :] = dsk_ref[...] * xs
        for p in range(pairs):
            cols = slice(LANES * p, LANES * (p + 1))
            xt_scr[cols, pl.ds(r0, CHUNK)] = xs[:, cols].T
        bs_scr[pl.ds(r0, CHUNK), :] = conv_silu(b_ref, cwb_ref, cbb_ref, c, r0).astype(BF16)
        cs_scr[pl.ds(r0, CHUNK), :] = conv_silu(c_ref, cwc_ref, cbc_ref, c, r0).astype(BF16)
        return carry

    def chunk_step(c, d):
        r0 = pl.multiple_of(c * CHUNK, CHUNK)
        xb = xb_scr[pl.ds(r0, CHUNK), :]
        bc = bs_scr[pl.ds(r0, CHUNK), :]
        cc = cs_scr[pl.ds(r0, CHUNK), :]
        acs_c = acsc_ref[pl.ds(r0, CHUNK), :]
        acs_r = acsr_ref[:, pl.ds(r0, CHUNK)]
        ldt_r = ldtr_ref[:, pl.ds(r0, CHUNK)]
        last = CHUNK - 1 if d == 0 else 0
        end_r = jnp.broadcast_to(acs_r[:, last:last + 1], acs_r.shape)
        w_r = jnp.exp2(end_r + ldt_r)
        tot_r = jnp.exp2(end_r)
        cb = lax.dot_general(cc, bc, NT_DIMS, preferred_element_type=F32)
        mask = (ti >= si) if d == 0 else (ti <= si)

        def head(k):
            col = jnp.broadcast_to(acs_c[:, k:k + 1], (CHUNK, CHUNK))
            decay = jnp.exp2(jnp.where(mask, col + ldt_r[k:k + 1, :], -jnp.inf))
            return col, (cb * decay).astype(BF16)

        for p in range(pairs):
            k0 = SSD_HEADS_PER_GROUP * d + 2 * p
            k1 = k0 + 1
            cols = slice(LANES * p, LANES * (p + 1))
            col0, m0 = head(k0)
            col1, m1 = head(k1)
            xbp = xb[:, cols]
            zero = jnp.zeros_like(xbp)
            y = jnp.dot(m0, jnp.where(lo, xbp, zero), preferred_element_type=F32)
            y += jnp.dot(m1, jnp.where(lo, zero, xbp), preferred_element_type=F32)
            hp = h_scr[d, cols, :]
            y_inter = lax.dot_general(cc, hp.astype(BF16), NT_DIMS, preferred_element_type=F32)
            y += y_inter * jnp.exp2(jnp.where(lo, col0, col1))
            xw = xt_scr[cols, pl.ds(r0, CHUNK)] * jnp.where(upper, w_r[k0:k0 + 1, :], w_r[k1:k1 + 1, :])
            s_new = jnp.dot(xw.astype(BF16), bc, preferred_element_type=F32)
            h_scr[d, cols, :] = jnp.where(upper, tot_r[k0:k0 + 1, :], tot_r[k1:k1 + 1, :]) * hp + s_new
            if d == 0:
                yf_scr[pl.ds(r0, CHUNK), cols] += y
            else:
                yb_scr[pl.ds(r0, CHUNK), cols] = y

    if has_h0:
        h_scr[...] = h0_ref[...]
    else:
        h_scr[...] = jnp.zeros_like(h_scr)
    lax.fori_loop(0, nc, conv_chunk, 0, unroll=2)

    def scan_body(i, carry):
        chunk_step(i, 0)
        chunk_step(nc - 1 - i, 1)
        return carry

    lax.fori_loop(0, nc, scan_body, 0, unroll=min(nc, 4))

    def finish(c, carry):
        r0 = pl.multiple_of(c * CHUNK, CHUNK)
        y = yf_scr[pl.ds(r0, CHUNK), :] + yb_scr[pl.ds(r0, CHUNK), :]
        g = y * _silu(z_ref[pl.ds(r0, CHUNK), :])
        y_ref[pl.ds(r0, CHUNK), :] = _rms(g, nw_ref[...]).astype(y_ref.dtype)
        return carry

    lax.fori_loop(0, nc, finish, 0, unroll=2)
    if has_ht:
        if n_prev:
            ht_ref[0:n_prev] = prev_ref[...]
        ht_ref[n_prev] = h_scr[...]


def _ssd_scan(proj, acs_c, ldt_r, acs_r, dsk, norm_w, conv_w, conv_b, row0, nseq, seq_len, h0, want_state,
              prev_states=None):
    n_prev = 0 if prev_states is None else prev_states.shape[1]
    groups = acs_c.shape[0]
    gw = SSD_HEADS_PER_GROUP * SSD_HEAD_DIM
    d_inner = groups * gw
    n = SSD_D_STATE
    nc = seq_len // CHUNK
    rb0 = row0 // seq_len
    xoff = d_inner // gw
    boff = 2 * d_inner // n
    coff = boff + groups
    cwx_off = 0
    cwb_off = d_inner // n
    cwc_off = cwb_off + groups

    in_specs = [
        pl.BlockSpec((seq_len, gw), lambda b, g: (rb0 + b, g)),
        pl.BlockSpec((seq_len, gw), lambda b, g: (rb0 + b, xoff + g)),
        pl.BlockSpec((seq_len, n), lambda b, g: (rb0 + b, boff + g)),
        pl.BlockSpec((seq_len, n), lambda b, g: (rb0 + b, coff + g)),
        pl.BlockSpec((None, seq_len, 2 * SSD_HEADS_PER_GROUP), lambda b, g: (g, rb0 + b, 0)),
        pl.BlockSpec((None, 2 * SSD_HEADS_PER_GROUP, seq_len), lambda b, g: (g, 0, rb0 + b)),
        pl.BlockSpec((None, 2 * SSD_HEADS_PER_GROUP, seq_len), lambda b, g: (g, 0, rb0 + b)),
        pl.BlockSpec((1, gw), lambda b, g: (0, g)),
        pl.BlockSpec((1, gw), lambda b, g: (0, g)),
        pl.BlockSpec((SSD_CONV_W, gw), lambda b, g: (0, cwx_off + g)),
        pl.BlockSpec((SSD_CONV_W, n), lambda b, g: (0, cwb_off + g)),
        pl.BlockSpec((SSD_CONV_W, n), lambda b, g: (0, cwc_off + g)),
        pl.BlockSpec((1, gw), lambda b, g: (0, cwx_off + g)),
        pl.BlockSpec((1, n), lambda b, g: (0, cwb_off + g)),
        pl.BlockSpec((1, n), lambda b, g: (0, cwc_off + g)),
    ]
    args = [proj, proj, proj, proj, acs_c, ldt_r, acs_r, dsk, norm_w,
            conv_w, conv_w, conv_w, conv_b, conv_b, conv_b]
    if h0 is not None:
        in_specs.append(pl.BlockSpec((None, 2, gw, n), lambda b, g: (b, 0, g, 0)))
        args.append(h0)
    if n_prev:
        in_specs.append(pl.BlockSpec((None, n_prev, 2, gw, n), lambda b, g: (b, 0, 0, g, 0)))
        args.append(prev_states)
    out_shape = [jax.ShapeDtypeStruct((nseq * seq_len, d_inner), BF16)]
    out_specs = [pl.BlockSpec((seq_len, gw), lambda b, g: (b, g))]
    if want_state:
        out_shape.append(jax.ShapeDtypeStruct((nseq, n_prev + 1, 2, d_inner, n), F32))
        out_specs.append(pl.BlockSpec((None, n_prev + 1, 2, gw, n), lambda b, g: (b, 0, 0, g, 0)))
    outs = pl.pallas_call(
        functools.partial(_ssd_kernel, nc, h0 is not None, want_state, n_prev),
        out_shape=out_shape,
        grid=(nseq, groups),
        in_specs=in_specs,
        out_specs=out_specs,
        scratch_shapes=[
            pltpu.VMEM((seq_len, gw), BF16),
            pltpu.VMEM((gw, seq_len), F32),
            pltpu.VMEM((seq_len, n), BF16),
            pltpu.VMEM((seq_len, n), BF16),
            pltpu.VMEM((seq_len, gw), F32),
            pltpu.VMEM((seq_len, gw), F32),
            pltpu.VMEM((2, gw, n), F32),
        ],
        compiler_params=_params("parallel", "parallel"),
        name="ssd_scan",
    )(*args)
    return outs if want_state else (outs[0], None)


def _ssd_mixer(x, norm_w_mix, mod5, layer, tok, geo, h0_sample, prev_states, j, w_in, conv_w, conv_b, dt_bias,
               a_log, d_skip, norm_w, w_out):
    heads = dt_bias.shape[1]
    groups = heads // SSD_HEADS_PER_GROUP
    d_inner = heads * SSD_HEAD_DIM
    hpg = SSD_HEADS_PER_GROUP
    t = x.shape[0]
    proj = _norm_mod_matmul(x, norm_w_mix, mod5, layer, tok, w_in, j)
    dt_off = d_inner + conv_w.shape[1]
    acs, acs_t, ldt_t = _ssd_dt(proj[:, dt_off:dt_off + 2 * heads], dt_bias, a_log)

    def rows_per_group(v):
        return v.reshape(2, groups, hpg, t).transpose(1, 0, 2, 3).reshape(groups, 2 * hpg, t)

    acs_c = acs.reshape(t, 2, groups, hpg).transpose(2, 0, 1, 3).reshape(groups, t, 2 * hpg)
    acs_r = rows_per_group(acs_t)
    ldt_r = rows_per_group(ldt_t)
    dsk = jnp.repeat(d_skip, SSD_HEAD_DIM).reshape(1, d_inner)
    nw = norm_w.reshape(1, d_inner)
    cb = conv_b.reshape(1, -1)
    bp, lp, bs, ls = geo
    y_p, states = _ssd_scan(proj, acs_c, ldt_r, acs_r, dsk, nw, conv_w, cb, 0, bp, lp, None, True, prev_states)
    h0 = h0_sample.reshape(bs, 2, d_inner, SSD_D_STATE)
    y_s, _ = _ssd_scan(proj, acs_c, ldt_r, acs_r, dsk, nw, conv_w, cb, bp * lp, bs, ls, h0, False)
    x = _matmul_residual(y_p, y_s, w_out, j, x, mod5, layer, tok)
    return x, states


def _rope(x, cos, sin):
    quarter = ATTN_HEAD_DIM // 4
    lane = lax.broadcasted_iota(jnp.int32, (1, ATTN_HEAD_DIM), 1)
    first = (lane % (2 * quarter)) < quarter
    swapped = jnp.where(first, pltpu.roll(x, ATTN_HEAD_DIM - quarter, 1), pltpu.roll(x, quarter, 1))
    return x * cos + swapped * sin


def _attn_kernel(nq, has_cache, use_rope, write_k, *refs):
    q_ref, k_ref, v_ref, qn_ref, kn_ref = refs[:5]
    pos = 5
    cos_ref = sin_ref = kc_ref = vc_ref = knew_ref = None
    if use_rope:
        cos_ref, sin_ref = refs[pos:pos + 2]
        pos += 2
    if has_cache:
        kc_ref, vc_ref = refs[pos:pos + 2]
        pos += 2
    o_ref = refs[pos]
    pos += 1
    if write_k:
        knew_ref = refs[pos]
        pos += 1
    kb_scr, vt_scr = refs[pos:pos + 2]
    pos += 2
    if has_cache:
        kcb_scr, vct_scr = refs[pos:pos + 2]
        kcb_scr[...] = kc_ref[...].astype(BF16)
        for c in range(vc_ref.shape[0] // CHUNK):
            rows = slice(c * CHUNK, (c + 1) * CHUNK)
            vct_scr[:, rows] = vc_ref[rows, :].T.astype(BF16)

    grp = q_ref.shape[1] // ATTN_HEAD_DIM
    scale = ATTN_HEAD_DIM ** -0.5

    def kprep(c, carry):
        r0 = pl.multiple_of(c * CHUNK, CHUNK)
        kn = _rms(k_ref[pl.ds(r0, CHUNK), :], kn_ref[...])
        if write_k:
            knew_ref[pl.ds(r0, CHUNK), :] = kn
        if use_rope:
            kn = _rope(kn, cos_ref[pl.ds(r0, CHUNK), :], sin_ref[pl.ds(r0, CHUNK), :])
        kb_scr[pl.ds(r0, CHUNK), :] = kn.astype(BF16)
        vt_scr[:, pl.ds(r0, CHUNK)] = v_ref[pl.ds(r0, CHUNK), :].T.astype(BF16)
        return carry

    lax.fori_loop(0, nq, kprep, 0)

    def qblock(c, carry):
        r0 = pl.multiple_of(c * CHUNK, CHUNK)
        q_t = []
        for g in range(grp):
            cols = slice(ATTN_HEAD_DIM * g, ATTN_HEAD_DIM * (g + 1))
            qn = _rms(q_ref[pl.ds(r0, CHUNK), cols], qn_ref[...])
            if use_rope:
                qn = _rope(qn, cos_ref[pl.ds(r0, CHUNK), :], sin_ref[pl.ds(r0, CHUNK), :])
            q_t.append((qn * (scale * LOG2_E)).T.astype(BF16))
        q_t = jnp.concatenate(q_t, axis=1)
        s = jnp.dot(kb_scr[...], q_t, preferred_element_type=F32)
        m = jnp.max(s, axis=0, keepdims=True)
        if has_cache:
            sc = jnp.dot(kcb_scr[...], q_t, preferred_element_type=F32)
            m = jnp.maximum(m, jnp.max(sc, axis=0, keepdims=True))
            ec = jnp.exp2(sc - m)
        e = jnp.exp2(s - m)
        den = jnp.sum(e, axis=0, keepdims=True)
        if has_cache:
            den = den + jnp.sum(ec, axis=0, keepdims=True)
        o_t = jnp.dot(vt_scr[...], e.astype(BF16), preferred_element_type=F32)
        if has_cache:
            o_t += jnp.dot(vct_scr[...], ec.astype(BF16), preferred_element_type=F32)
        o_t = o_t * (1.0 / den)
        for g in range(grp):
            cols = slice(ATTN_HEAD_DIM * g, ATTN_HEAD_DIM * (g + 1))
            o_ref[pl.ds(r0, CHUNK), cols] = o_t[:, cols].T.astype(o_ref.dtype)
        return carry

    lax.fori_loop(0, nq, qblock, 0, unroll=min(nq, 4))


def _attention(qkv, q_norm, k_norm, heads, kv_heads, row0, nseq, seq_len, rope, cache, write_k):
    dh = ATTN_HEAD_DIM
    grp = heads // kv_heads
    rb0 = row0 // seq_len
    nq = seq_len // CHUNK
    in_specs = [
        pl.BlockSpec((seq_len, grp * dh), lambda b, h: (rb0 + b, h)),
        pl.BlockSpec((seq_len, dh), lambda b, h: (rb0 + b, heads + h)),
        pl.BlockSpec((seq_len, dh), lambda b, h: (rb0 + b, heads + kv_heads + h)),
        pl.BlockSpec((1, dh), lambda b, h: (0, 0)),
        pl.BlockSpec((1, dh), lambda b, h: (0, 0)),
    ]
    args = [qkv, qkv, qkv, q_norm.reshape(1, dh), k_norm.reshape(1, dh)]
    scratch = [pltpu.VMEM((seq_len, dh), BF16), pltpu.VMEM((dh, seq_len), BF16)]
    if rope is not None:
        in_specs += [pl.BlockSpec((seq_len, dh), lambda b, h: (0, 0))] * 2
        args += list(rope)
    if cache is not None:
        past = cache[0].shape[1]
        in_specs += [pl.BlockSpec((None, past, dh), lambda b, h: (b, 0, h))] * 2
        args += list(cache)
        scratch += [pltpu.VMEM((past, dh), BF16), pltpu.VMEM((dh, past), BF16)]
    out_shape = [jax.ShapeDtypeStruct((nseq * seq_len, heads * dh), BF16)]
    out_specs = [pl.BlockSpec((seq_len, grp * dh), lambda b, h: (b, h))]
    if write_k:
        out_shape.append(jax.ShapeDtypeStruct((nseq * seq_len, kv_heads * dh), F32))
        out_specs.append(pl.BlockSpec((seq_len, dh), lambda b, h: (b, h)))
    outs = pl.pallas_call(
        functools.partial(_attn_kernel, nq, cache is not None, rope is not None, write_k),
        out_shape=out_shape,
        grid=(nseq, kv_heads),
        in_specs=in_specs,
        out_specs=out_specs,
        scratch_shapes=scratch,
        compiler_params=_params("parallel", "parallel"),
        name="attention",
    )(*args)
    return outs if write_k else (outs[0], None)


def _rope_tables(seq_len):
    quarter = ATTN_HEAD_DIM // 4
    t = jnp.arange(seq_len)
    lane = jnp.arange(ATTN_HEAD_DIM)
    pos = jnp.where(lane[None, :] < 2 * quarter, (t // GRID_W)[:, None], (t % GRID_W)[:, None]).astype(F32)
    inv_freq = ROPE_THETA ** (-jnp.arange(quarter, dtype=F32) / quarter)
    ang = pos * inv_freq[lane % quarter][None, :]
    sign = jnp.where((lane % (2 * quarter)) < quarter, -1.0, 1.0).astype(F32)
    return jnp.cos(ang), jnp.sin(ang) * sign[None, :]


def _attn_mixer(x, norm_w_mix, mod5, layer, tok, geo, cache_k, cache_v, j, w_qkv, q_norm, k_norm, w_out):
    bp, lp, bs, ls = geo
    dh = ATTN_HEAD_DIM
    kv_heads = cache_k.shape[2]
    heads = w_out.shape[1] // dh
    qkv = _norm_mod_matmul(x, norm_w_mix, mod5, layer, tok, w_qkv, j)
    o_p, k_new = _attention(qkv, q_norm, k_norm, heads, kv_heads, 0, bp, lp, None, None, True)
    past = cache_k.shape[1]
    cache = (cache_k.reshape(bs, past, kv_heads * dh), cache_v.reshape(bs, past, kv_heads * dh))
    o_s, _ = _attention(qkv, q_norm, k_norm, heads, kv_heads, bp * lp, bs, ls, _rope_tables(ls), cache, False)
    x = _matmul_residual(o_p, o_s, w_out, j, x, mod5, layer, tok)
    v_new = qkv[:bp * lp, (heads + kv_heads) * dh:]
    return x, k_new.reshape(bp, lp, kv_heads, dh), v_new.reshape(bp, lp, kv_heads, dh)


def _mlstm_gate_kernel(heads, i_ref, f_ref, bi_ref, bf_ref, b_ref, r_ref, p_ref):
    forward = lax.broadcasted_iota(jnp.int32, (1, i_ref.shape[1]), 1) < heads
    for c in range(i_ref.shape[0] // CHUNK):
        rows = slice(c * CHUNK, (c + 1) * CHUNK)
        log_f = _log_sigmoid(f_ref[rows, :] + bf_ref[...])
        b = jnp.where(forward, _cumsum_chunk(log_f, False), _cumsum_chunk(log_f, True))
        r = i_ref[rows, :] + bi_ref[...] - b
        b_ref[rows, :] = b
        r_ref[rows, :] = r
        p_ref[rows, :] = jnp.where(forward, _cummax_chunk(r, False), _cummax_chunk(r, True))


def _mlstm_gates(i_raw, f_raw, b_i, b_f):
    t, cols = i_raw.shape
    tm = _pick_tile(t, (1024, 512, 256, 128))
    return pl.pallas_call(
        functools.partial(_mlstm_gate_kernel, cols // 2),
        out_shape=[jax.ShapeDtypeStruct((t, cols), F32)] * 3,
        grid=(t // tm,),
        in_specs=[
            pl.BlockSpec((tm, cols), lambda i: (i, 0)),
            pl.BlockSpec((tm, cols), lambda i: (i, 0)),
            pl.BlockSpec((1, cols), lambda i: (0, 0)),
            pl.BlockSpec((1, cols), lambda i: (0, 0)),
        ],
        out_specs=[pl.BlockSpec((tm, cols), lambda i: (i, 0))] * 3,
        compiler_params=_params("parallel"),
        name="mlstm_gates",
    )(i_raw, f_raw, b_i, b_f)


def _mlstm_kernel(nc, has_s0, has_st, *refs):
    q_ref, k_ref, v_ref, og_ref, cp_ref, rr_ref, hn_ref = refs[:7]
    pos = 7
    c0_ref = n0_ref = m0_ref = ct_ref = nt_ref = mt_ref = None
    if has_s0:
        c0_ref, n0_ref, m0_ref = refs[pos:pos + 3]
        pos += 3
    h_ref = refs[pos]
    pos += 1
    if has_st:
        ct_ref, nt_ref, mt_ref = refs[pos:pos + 3]
        pos += 3
    hf_scr, hb_scr, c_scr, n_scr, m_scr = refs[pos:]

    dk = q_ref.shape[1]
    dv = v_ref.shape[1]
    qscale = dk ** -0.5
    tile = (CHUNK, CHUNK)
    ti = lax.broadcasted_iota(jnp.int32, tile, 0)
    si = lax.broadcasted_iota(jnp.int32, tile, 1)

    def wide(x):
        return jnp.concatenate([x] * (dv // CHUNK), axis=1)

    def chunk_step(c, d):
        r0 = pl.multiple_of(c * CHUNK, CHUNK)
        q = q_ref[pl.ds(r0, CHUNK), :] * qscale
        k = k_ref[pl.ds(r0, CHUNK), :]
        qb = q.astype(BF16)
        kb = k.astype(BF16)
        vb = v_ref[pl.ds(r0, CHUNK), :].astype(BF16)
        cp = cp_ref[pl.ds(r0, CHUNK), :]
        b_t = jnp.broadcast_to(cp[:, 3 * d:3 * d + 1], tile)
        r_t = jnp.broadcast_to(cp[:, 3 * d + 1:3 * d + 2], tile)
        p_t = jnp.broadcast_to(cp[:, 3 * d + 2:3 * d + 3], tile)
        r_row = rr_ref[d:d + 1, pl.ds(r0, CHUNK)]
        m_row = m_scr[d]
        u_t = jnp.maximum(m_row, p_t)
        mask = (ti >= si) if d == 0 else (ti <= si)
        s = lax.dot_general(qb, kb, NT_DIMS, preferred_element_type=F32)
        s = s * jnp.exp(jnp.where(mask, r_row - u_t, -jnp.inf))
        inter = jnp.exp(m_row - u_t)
        cst = c_scr[d]
        nst = n_scr[d]
        num = jnp.dot(s.astype(BF16), vb, preferred_element_type=F32)
        num += wide(inter) * jnp.dot(qb, cst.astype(BF16), preferred_element_type=F32)
        den = jnp.sum(s + inter * (q * nst), axis=1, keepdims=True)
        rcp = 1.0 / jnp.maximum(jnp.abs(den), jnp.exp(-(b_t + u_t)))
        h = num * wide(rcp)
        last = CHUNK - 1 if d == 0 else 0
        u_last = u_t[last:last + 1, :]
        wk = jnp.exp(r_t - u_last) * k
        carry_decay = jnp.exp(m_row - u_last)
        c_scr[d] = wide(carry_decay) * cst + jnp.dot(wk.T.astype(BF16), vb, preferred_element_type=F32)
        n_scr[d] = carry_decay * nst + jnp.sum(wk, axis=0, keepdims=True)
        m_scr[d] = b_t[last:last + 1, :] + u_last
        (hf_scr if d == 0 else hb_scr)[pl.ds(r0, CHUNK), :] = h

    for d in range(2):
        if has_s0:
            c_scr[d] = c0_ref[d]
            n_scr[d] = n0_ref[d]
            m_scr[d] = m0_ref[d]
        else:
            c_scr[d] = jnp.zeros(c_scr.shape[1:], F32)
            n_scr[d] = jnp.zeros(n_scr.shape[1:], F32)
            m_scr[d] = jnp.zeros(m_scr.shape[1:], F32)

    def scan_body(i, carry):
        chunk_step(i, 0)
        chunk_step(nc - 1 - i, 1)
        return carry

    lax.fori_loop(0, nc, scan_body, 0, unroll=min(nc, 4))

    def finish(c, carry):
        r0 = pl.multiple_of(c * CHUNK, CHUNK)
        hs = _rms(hf_scr[pl.ds(r0, CHUNK), :] + hb_scr[pl.ds(r0, CHUNK), :], hn_ref[...])
        h_ref[pl.ds(r0, CHUNK), :] = (hs * _sigmoid(og_ref[pl.ds(r0, CHUNK), :])).astype(h_ref.dtype)
        return carry

    lax.fori_loop(0, nc, finish, 0, unroll=2)
    if has_st:
        for d in range(2):
            ct_ref[d] = c_scr[d]
            nt_ref[d] = n_scr[d]
            mt_ref[d] = m_scr[d]


def _mlstm_scan(proj, gate_cols, gate_rows, head_norm, heads, dk, dv, row0, nseq, seq_len, state, want_state):
    assert dv == 2 * dk and dk == CHUNK
    nc = seq_len // CHUNK
    rb0 = row0 // seq_len
    in_specs = [
        pl.BlockSpec((seq_len, dk), lambda b, h: (rb0 + b, h)),
        pl.BlockSpec((seq_len, dk), lambda b, h: (rb0 + b, heads + h)),
        pl.BlockSpec((seq_len, dv), lambda b, h: (rb0 + b, heads + h)),
        pl.BlockSpec((seq_len, dv), lambda b, h: (rb0 + b, 2 * heads + h)),
        pl.BlockSpec((None, seq_len, 6), lambda b, h: (h, rb0 + b, 0)),
        pl.BlockSpec((None, 2, seq_len), lambda b, h: (h, 0, rb0 + b)),
        pl.BlockSpec((None, 1, dv), lambda b, h: (h, 0, 0)),
    ]
    args = [proj, proj, proj, proj, gate_cols, gate_rows, head_norm]
    state_specs = [
        pl.BlockSpec((None, 2, None, dk, dv), lambda b, h: (b, 0, h, 0, 0)),
        pl.BlockSpec((None, 2, None, 1, dk), lambda b, h: (b, 0, h, 0, 0)),
        pl.BlockSpec((None, 2, None, 1, LANES), lambda b, h: (b, 0, h, 0, 0)),
    ]
    if state is not None:
        in_specs += state_specs
        args += list(state)
    out_shape = [jax.ShapeDtypeStruct((nseq * seq_len, heads * dv), BF16)]
    out_specs = [pl.BlockSpec((seq_len, dv), lambda b, h: (b, h))]
    if want_state:
        out_shape += [jax.ShapeDtypeStruct((nseq, 2, heads, dk, dv), F32),
                      jax.ShapeDtypeStruct((nseq, 2, heads, 1, dk), F32),
                      jax.ShapeDtypeStruct((nseq, 2, heads, 1, LANES), F32)]
        out_specs += state_specs
    outs = pl.pallas_call(
        functools.partial(_mlstm_kernel, nc, state is not None, want_state),
        out_shape=out_shape,
        grid=(nseq, heads),
        in_specs=in_specs,
        out_specs=out_specs,
        scratch_shapes=[
            pltpu.VMEM((seq_len, dv), F32),
            pltpu.VMEM((seq_len, dv), F32),
            pltpu.VMEM((2, dk, dv), F32),
            pltpu.VMEM((2, 1, dk), F32),
            pltpu.VMEM((2, 1, LANES), F32),
        ],
        compiler_params=_params("parallel", "parallel"),
        name="mlstm_scan",
    )(*args)
    return outs


def _mlstm_mixer(x, norm_w_mix, mod5, layer, tok, geo, c0, n0, m0, j, w_in, b_gates, head_norm, w_out):
    bp, lp, bs, ls = geo
    heads, dk, dv = c0.shape[2], c0.shape[3], c0.shape[4]
    t = x.shape[0]
    proj = _norm_mod_matmul(x, norm_w_mix, mod5, layer, tok, w_in, j)
    g_off = 2 * heads * dk + 2 * heads * dv
    graw = proj[:, g_off:g_off + 4 * heads].reshape(t, 2, 2, heads)
    b, r, p = _mlstm_gates(graw[:, :, 0, :].reshape(t, 2 * heads), graw[:, :, 1, :].reshape(t, 2 * heads),
                           b_gates[:, 0, :].reshape(1, 2 * heads), b_gates[:, 1, :].reshape(1, 2 * heads))
    gate_cols = jnp.stack([b, r, p], axis=-1).reshape(t, 2, heads, 3).transpose(2, 0, 1, 3).reshape(heads, t, 6)
    gate_rows = r.reshape(t, 2, heads).transpose(2, 1, 0)
    hn = head_norm.reshape(heads, 1, dv)
    outs_p = _mlstm_scan(proj, gate_cols, gate_rows, hn, heads, dk, dv, 0, bp, lp, None, True)
    h_p, c_t, n_t, m_t = outs_p
    state = (c0, n0.reshape(bs, 2, heads, 1, dk),
             jnp.broadcast_to(m0[..., None, None], (bs, 2, heads, 1, LANES)))
    (h_s,) = _mlstm_scan(proj, gate_cols, gate_rows, hn, heads, dk, dv, bp * lp, bs, ls, state, False)
    x = _matmul_residual(h_p, h_s, w_out, j, x, mod5, layer, tok)
    return x, c_t, n_t.reshape(bp, 2, heads, dk), m_t[:, :, :, 0, 0]


def kernel(x_prompt, x_sample, c, state_ssd, cache_attn_k, cache_attn_v, state_mlstm_C, state_mlstm_n, state_mlstm_m, c_ctx, ada_w, ada_b, norm_mix_w, norm_ffn_w, ffn_w_gate, ffn_w_up, ffn_w_down, ssd_w_in, ssd_conv_w, ssd_conv_b, ssd_dt_bias, ssd_a_log, ssd_d, ssd_norm_w, ssd_w_out, attn_w_qkv, attn_q_norm, attn_k_norm, attn_w_out, mlstm_w_in, mlstm_b_gates, mlstm_head_norm, mlstm_w_out):
    bp, lp, d = x_prompt.shape
    bs, ls, _ = x_sample.shape
    depth = ada_w.shape[0]
    geo = (bp, lp, bs, ls)
    tok = _Tokens(bp * lp, ls)
    x = jnp.concatenate([x_prompt.reshape(bp * lp, d), x_sample.reshape(bs * ls, d)], axis=0)

    mod_rows = -(-(1 + bs) // SUBLANES) * SUBLANES
    cc = jnp.concatenate([c_ctx[None, :], c, jnp.zeros((mod_rows - 1 - bs, d), F32)], axis=0)
    mod5 = _modulation(cc, ada_w, ada_b).reshape(depth, mod_rows, 6, 1, d)

    w_gate, w_up, w_down = _to_bf16(ffn_w_gate), _to_bf16(ffn_w_up), _to_bf16(ffn_w_down)
    ssd_in, ssd_out = _to_bf16(ssd_w_in, 6 * MXU_COLS), _to_bf16(ssd_w_out)
    attn_qkv, attn_out = _to_bf16(attn_w_qkv), _to_bf16(attn_w_out)
    mlstm_in = _to_bf16(mlstm_w_in, 5 * MXU_COLS)
    mlstm_out = _to_bf16(mlstm_w_out)
    ssd_states = None
    new_k, new_v, new_c, new_n, new_m = [], [], [], [], []
    for l in range(depth):
        kind, j = l % 3, l // 3
        if kind == 0:
            x, ssd_states = _ssd_mixer(x, norm_mix_w[l], mod5, l, tok, geo, state_ssd[:, j], ssd_states, j, ssd_in,
                                       ssd_conv_w[j], ssd_conv_b[j], ssd_dt_bias[j], ssd_a_log[j], ssd_d[j],
                                       ssd_norm_w[j], ssd_out)
        elif kind == 1:
            x, k_new, v_new = _attn_mixer(x, norm_mix_w[l], mod5, l, tok, geo, cache_attn_k[:, j], cache_attn_v[:, j],
                                          j, attn_qkv, attn_q_norm[j], attn_k_norm[j], attn_out)
            new_k.append(k_new)
            new_v.append(v_new)
        else:
            x, c_t, n_t, m_t = _mlstm_mixer(x, norm_mix_w[l], mod5, l, tok, geo, state_mlstm_C[:, j],
                                            state_mlstm_n[:, j], state_mlstm_m[:, j], j, mlstm_in,
                                            mlstm_b_gates[j], mlstm_head_norm[j], mlstm_out)
            new_c.append(c_t)
            new_n.append(n_t)
            new_m.append(m_t)
        x = _ffn(x, norm_ffn_w[l], mod5, l, tok, w_gate, w_up, w_down, split_out=(l == depth - 1))

    y_prompt = x[0].reshape(bp, lp, d)
    y_sample = x[1].reshape(bs, ls, d)
    ssd_states = ssd_states.reshape((bp,) + state_ssd.shape[1:])
    return (y_prompt, y_sample, ssd_states, jnp.stack(new_k, axis=1), jnp.stack(new_v, axis=1),
            jnp.stack(new_c, axis=1), jnp.stack(new_n, axis=1), jnp.stack(new_m, axis=1))
```
